```python
import math
import jax
import jax.numpy as jnp
from jax import lax
import numpy as np

D_MODEL = 2048
BATCH = 4
SEQ = 4096
DEPTH = 1

D_SSM = D_MODEL // 2
D_ATT = D_MODEL - D_SSM
SSM_HEADDIM = 64
SSM_HEADS = D_SSM // SSM_HEADDIM
SSM_GROUPS = 2
SSM_HPG = SSM_HEADS // SSM_GROUPS
SSM_STATE = 128
CONV_WIDTH = 4
CHUNK = 128
D_CONV = D_SSM + 2 * SSM_GROUPS * SSM_STATE
ATT_HEADDIM = 64
ATT_HEADS = D_ATT // ATT_HEADDIM
ATT_KV_HEADS = 2
ATT_QPK = ATT_HEADS // ATT_KV_HEADS
WINDOW = 128
D_KV = ATT_KV_HEADS * ATT_HEADDIM
PEER_HEADS = 8
PEER_KEYS = 128
PEER_EXPERTS = PEER_KEYS * PEER_KEYS
PEER_TOPK = 16
PEER_QDIM = 256
PEER_HALF = PEER_QDIM // 2
PEER_TOKEN_BLOCK = 128
N_MOD = 6
ALPHA = (2.0 * DEPTH) ** 0.25
BETA = (8.0 * DEPTH) ** -0.25
EPS = 1e-5
SPLIT_IDX = (D_SSM,
             D_SSM + D_CONV,
             D_SSM + D_CONV + SSM_HEADS,
             D_SSM + D_CONV + SSM_HEADS + D_ATT,
             D_SSM + D_CONV + SSM_HEADS + D_ATT + D_KV)
D_IN = SPLIT_IDX[-1] + D_KV

kernel_name = "hybrid_ssd_swa_peer_deepnorm_adaln"


def layer_norm(x, g, b):
    xf = x.astype(jnp.float32)
    mu = jnp.mean(xf, axis=-1, keepdims=True)
    var = jnp.mean(jnp.square(xf - mu), axis=-1, keepdims=True)
    return ((xf - mu) * lax.rsqrt(var + EPS) * g.astype(jnp.float32) + b.astype(jnp.float32)).astype(x.dtype)


def rms_norm(x, g):
    xf = x.astype(jnp.float32)
    return xf * lax.rsqrt(jnp.mean(jnp.square(xf), axis=-1, keepdims=True) + EPS) * g.astype(jnp.float32)


def causal_dwconv(x, w, b):
    out = lax.conv_general_dilated(
        x, w[:, None, :].astype(x.dtype), window_strides=(1,), padding=[(CONV_WIDTH - 1, 0)],
        dimension_numbers=('NWC', 'WIO', 'NWC'), feature_group_count=x.shape[-1])
    return out + b.astype(x.dtype)


def ssd_chunked(xh, dt, A, Bm, Cm):
    b, s = xh.shape[:2]
    nc = s // CHUNK
    xh = xh.reshape(b, nc, CHUNK, SSM_GROUPS, SSM_HPG, SSM_HEADDIM)
    dt = dt.reshape(b, nc, CHUNK, SSM_GROUPS, SSM_HPG)
    Bm = Bm.reshape(b, nc, CHUNK, SSM_GROUPS, SSM_STATE)
    Cm = Cm.reshape(b, nc, CHUNK, SSM_GROUPS, SSM_STATE)
    a_cum = jnp.cumsum(dt * A, axis=2)
    a_t = jnp.moveaxis(a_cum, 2, -1)
    seg = a_t[..., :, None] - a_t[..., None, :]
    causal = jnp.tril(jnp.ones((CHUNK, CHUNK), dtype=bool))
    decay = jnp.exp(jnp.where(causal, seg, -jnp.inf))
    cb = jnp.einsum('bclgn,bcsgn->bcgls', Cm, Bm)
    w = cb[:, :, :, None] * decay * jnp.moveaxis(dt, 2, -1)[..., None, :]
    y_diag = jnp.einsum('bcgrls,bcsgrp->bclgrp', w, xh)
    decay_to_end = jnp.exp(a_cum[:, :, -1:] - a_cum)
    states = jnp.einsum('bclgn,bclgr,bclgrp->bcgrpn', Bm, decay_to_end * dt, xh)
    chunk_decay = jnp.exp(a_cum[:, :, -1])

    def step(h, inp):
        st, dec = inp
        return dec[..., None, None] * h + st, h

    h0 = jnp.zeros((b, SSM_GROUPS, SSM_HPG, SSM_HEADDIM, SSM_STATE), jnp.float32)
    _, prev = lax.scan(step, h0, (jnp.moveaxis(states, 1, 0), jnp.moveaxis(chunk_decay, 1, 0)))
    prev = jnp.moveaxis(prev, 0, 1)
    y_off = jnp.einsum('bclgn,bcgrpn,bclgr->bclgrp', Cm, prev, jnp.exp(a_cum))
    return (y_diag + y_off).reshape(b, s, SSM_GROUPS, SSM_HPG, SSM_HEADDIM)


def sliding_window_sink_attention(q, k, v, sinks):
    f32 = jnp.float32
    b, s, _ = q.shape
    nb = s // WINDOW
    qh = q.astype(f32).reshape(b, nb, WINDOW, ATT_KV_HEADS, ATT_QPK, ATT_HEADDIM) * (ATT_HEADDIM ** -0.5)
    kh = k.astype(f32).reshape(b, nb, WINDOW, ATT_KV_HEADS, ATT_HEADDIM)
    vh = v.astype(f32).reshape(b, nb, WINDOW, ATT_KV_HEADS, ATT_HEADDIM)

    def with_prev(t):
        prev = jnp.concatenate([jnp.zeros_like(t[:, :1]), t[:, :-1]], axis=1)
        return jnp.concatenate([prev, t], axis=2)

    kb, vb = with_prev(kh), with_prev(vh)
    scores = jnp.einsum('bnqhrd,bnshd->bnhrqs', qh, kb)
    qi = jnp.arange(WINDOW)[:, None]
    kj = jnp.arange(2 * WINDOW)[None, :]
    dist = qi + WINDOW - kj
    band = (dist >= 0) & (dist < WINDOW)
    block_ids = jnp.arange(nb)[:, None, None]
    mask = band[None] & ((block_ids > 0) | (kj >= WINDOW)[None])
    scores = jnp.where(mask[None, :, None, None], scores, -jnp.inf)
    sink = sinks.astype(f32).reshape(ATT_KV_HEADS, ATT_QPK)[None, None, :, :, None, None]
    m = jnp.maximum(jnp.max(scores, axis=-1, keepdims=True), sink)
    p = jnp.exp(scores - m)
    denom = jnp.sum(p, axis=-1, keepdims=True) + jnp.exp(sink - m)
    o = jnp.einsum('bnhrqs,bnshd->bnqhrd', p / denom, vb)
    return o.reshape(b, s, D_ATT)


def hybrid_mixer(h, w_in, conv_w, conv_b, dt_bias, a_log, d_skip, ssm_norm_g, sinks, att_norm_g, w_out):
    f32 = jnp.float32
    b, s, _ = h.shape
    proj = h @ w_in
    z, xbc, dt_raw, q, k, v = jnp.split(proj, list(SPLIT_IDX), axis=-1)
    xbc = jax.nn.silu(causal_dwconv(xbc, conv_w, conv_b))
    xs, Bm, Cm = jnp.split(xbc, [D_SSM, D_SSM + SSM_GROUPS * SSM_STATE], axis=-1)
    dt = jax.nn.softplus(dt_raw.astype(f32) + dt_bias.astype(f32))
    A = -jnp.exp(a_log.astype(f32))
    xh = xs.astype(f32).reshape(b, s, SSM_GROUPS, SSM_HPG, SSM_HEADDIM)
    y = ssd_chunked(xh, dt.reshape(b, s, SSM_GROUPS, SSM_HPG), A.reshape(SSM_GROUPS, SSM_HPG),
                    Bm.astype(f32).reshape(b, s, SSM_GROUPS, SSM_STATE),
                    Cm.astype(f32).reshape(b, s, SSM_GROUPS, SSM_STATE))
    y = y + d_skip.astype(f32).reshape(SSM_GROUPS, SSM_HPG)[..., None] * xh
    y_ssd = rms_norm(y.reshape(b, s, D_SSM) * jax.nn.silu(z.astype(f32)), ssm_norm_g)
    y_att = rms_norm(sliding_window_sink_attention(q, k, v, sinks), att_norm_g)
    return jnp.concatenate([y_ssd, y_att], axis=-1).astype(h.dtype) @ w_out


def peer(h, w_q, sub_keys, u_tab, v_tab):
    f32 = jnp.float32
    b, s, d = h.shape
    blocks = h.reshape(-1, PEER_TOKEN_BLOCK, d)
    keys = sub_keys.astype(f32)

    def block(hb):
        t = hb.shape[0]
        q = (hb @ w_q).astype(f32).reshape(t, PEER_HEADS, 2, PEER_HALF)
        sc = jnp.einsum('thcd,hcnd->thcn', q, keys)
        val, idx = lax.top_k(sc, PEER_TOPK)
        cand = val[..., 0, :, None] + val[..., 1, None, :]
        cand_idx = idx[..., 0, :, None] * PEER_KEYS + idx[..., 1, None, :]
        best, pos = lax.top_k(cand.reshape(t, PEER_HEADS, PEER_TOPK * PEER_TOPK), PEER_TOPK)
        expert = jnp.take_along_axis(cand_idx.reshape(t, PEER_HEADS, PEER_TOPK * PEER_TOPK), pos, axis=-1)
        g = jax.nn.softmax(best, axis=-1)
        u = u_tab[expert]
        a = jnp.einsum('td,thkd->thk', hb, u)
        wgt = (g * jax.nn.gelu(a.astype(f32), approximate=False)).astype(hb.dtype)
        return jnp.einsum('thk,thkd->td', wgt, v_tab[expert])

    return lax.map(block, blocks).reshape(b, s, d)


def setup_inputs(seed: int = 0) -> dict:
    key = jax.random.key(seed)
    ks = jax.random.split(key, 24)
    f32 = jnp.float32
    L = DEPTH

    def nrm(k, shape, scale):
        return jax.random.normal(k, shape, f32) * scale

    x = nrm(ks[0], (BATCH, SEQ, D_MODEL), 1.0)
    c = nrm(ks[1], (BATCH, D_MODEL), 1.0)
    w_ada = nrm(ks[2], (L, D_MODEL, N_MOD * D_MODEL), 0.5 * D_MODEL ** -0.5)
    b_ada = nrm(ks[3], (L, N_MOD * D_MODEL), 0.01)
    w_in = nrm(ks[4], (L, D_MODEL, D_IN), D_MODEL ** -0.5)
    w_in = w_in.at[:, :, SPLIT_IDX[-1]:].multiply(BETA)
    conv_w = nrm(ks[5], (L, CONV_WIDTH, D_CONV), CONV_WIDTH ** -0.5)
    conv_b = nrm(ks[6], (L, D_CONV), 0.02)
    dt0 = jnp.exp(jax.random.uniform(ks[7], (L, SSM_HEADS), f32, math.log(1e-3), math.log(1e-1)))
    dt_bias = dt0 + jnp.log(-jnp.expm1(-dt0))
    a_log = jnp.log(jax.random.uniform(ks[8], (L, SSM_HEADS), f32, 1.0, 16.0))
    d_skip = 1.0 + nrm(ks[9], (L, SSM_HEADS), 0.02)
    ssm_norm_g = 1.0 + nrm(ks[10], (L, D_SSM), 0.02)
    attn_sinks = nrm(ks[11], (L, ATT_HEADS), 0.5)
    attn_norm_g = 1.0 + nrm(ks[12], (L, D_ATT), 0.02)
    w_out = nrm(ks[13], (L, D_SSM + D_ATT, D_MODEL), BETA * (D_SSM + D_ATT) ** -0.5)
    ln1_g = 1.0 + nrm(ks[14], (L, D_MODEL), 0.02)
    ln1_b = nrm(ks[15], (L, D_MODEL), 0.02)
    peer_w_q = nrm(ks[16], (L, D_MODEL, PEER_HEADS * PEER_QDIM), D_MODEL ** -0.5)
    peer_sub_keys = nrm(ks[17], (L, PEER_HEADS, 2, PEER_KEYS, PEER_HALF), PEER_HALF ** -0.5)
    peer_u = nrm(ks[18], (L, PEER_EXPERTS, D_MODEL), D_MODEL ** -0.5)
    peer_v = nrm(ks[19], (L, PEER_EXPERTS, D_MODEL), BETA)
    ln2_g = 1.0 + nrm(ks[20], (L, D_MODEL), 0.02)
    ln2_b = nrm(ks[21], (L, D_MODEL), 0.02)
    return {'x': x, 'c': c, 'w_ada': w_ada, 'b_ada': b_ada, 'w_in': w_in, 'conv_w': conv_w,
            'conv_b': conv_b, 'dt_bias': dt_bias, 'a_log': a_log, 'd_skip': d_skip,
            'ssm_norm_g': ssm_norm_g, 'attn_sinks': attn_sinks, 'attn_norm_g': attn_norm_g,
            'w_out': w_out, 'ln1_g': ln1_g, 'ln1_b': ln1_b, 'peer_w_q': peer_w_q,
            'peer_sub_keys': peer_sub_keys, 'peer_u': peer_u, 'peer_v': peer_v,
            'ln2_g': ln2_g, 'ln2_b': ln2_b}


def reference(x, c, w_ada, b_ada, w_in, conv_w, conv_b, dt_bias, a_log, d_skip, ssm_norm_g,
              attn_sinks, attn_norm_g, w_out, ln1_g, ln1_b, peer_w_q, peer_sub_keys, peer_u,
              peer_v, ln2_g, ln2_b):
    for l in range(DEPTH):
        mod = jax.nn.silu(c) @ w_ada[l] + b_ada[l]
        shift1, scale1, gate1, shift2, scale2, gate2 = [m[:, None, :] for m in jnp.split(mod, N_MOD, axis=-1)]
        h = x * (1.0 + scale1) + shift1
        y = hybrid_mixer(h, w_in[l], conv_w[l], conv_b[l], dt_bias[l], a_log[l], d_skip[l],
                         ssm_norm_g[l], attn_sinks[l], attn_norm_g[l], w_out[l])
        x = layer_norm(ALPHA * x + gate1 * y, ln1_g[l], ln1_b[l])
        h = x * (1.0 + scale2) + shift2
        y = peer(h, peer_w_q[l], peer_sub_keys[l], peer_u[l], peer_v[l])
        x = layer_norm(ALPHA * x + gate2 * y, ln2_g[l], ln2_b[l])
    return x
```

```python
import functools

import jax
import jax.numpy as jnp
from jax import lax
from jax.experimental import pallas as pl
from jax.experimental.pallas import tpu as pltpu

f32 = jnp.float32
bf16 = jnp.bfloat16
i32 = jnp.int32
u32 = jnp.uint32

D_MODEL = 2048
D_SSM = 1024
D_ATT = 1024
SSM_HEADS = 16
SSM_GROUPS = 2
SSM_STATE = 128
HEADDIM = 64
CONV_WIDTH = 4
CHUNK = 128
D_BC = 2 * SSM_GROUPS * SSM_STATE
D_CONV = D_SSM + D_BC
ATT_HEADS = 16
ATT_KV_HEADS = 2
D_KV = ATT_KV_HEADS * HEADDIM
PEER_HEADS = 8
PEER_KEYS = 128
PEER_TOPK = 16
PEER_HALF = 128
PEER_SEL = PEER_HEADS * PEER_TOPK
N_MOD = 6
DEPTH = 1
ALPHA = (2.0 * DEPTH) ** 0.25
EPS = 1e-5

LANES = 128
SUBLANES = 8
D_PROJ = 4096
HALF_D = D_MODEL // 2
NEG_INF = float("-inf")


def _silu(x):
    return x * jax.nn.sigmoid(x)


def _gelu(x):
    return 0.5 * x * (1.0 + lax.erf(x * (2.0 ** -0.5)))


def _dot(a, b):
    return jnp.dot(a, b, preferred_element_type=f32)


def _dot_nt(a, b):
    return lax.dot_general(a, b, (((1,), (1,)), ((), ())), preferred_element_type=f32)


def _layer_norm(x, g, b):
    mu = jnp.mean(x, axis=-1, keepdims=True)
    xc = x - mu
    var = jnp.mean(xc * xc, axis=-1, keepdims=True)
    return xc * lax.rsqrt(var + EPS) * g + b


def _rms_norm(x, g):
    return x * lax.rsqrt(jnp.mean(x * x, axis=-1, keepdims=True) + EPS) * g


ADA_TN = 1024


def _ada_kernel(c_ref, w_ref, b_ref, o_ref):
    sc = _silu(c_ref[...])
    o_ref[...] = _dot(sc.astype(bf16), w_ref[...].astype(bf16)) + b_ref[...]


def _ada(c_pad, w_ada, b_ada):
    n = w_ada.shape[1]
    return pl.pallas_call(
        _ada_kernel,
        grid=(n // ADA_TN,),
        in_specs=[pl.BlockSpec((SUBLANES, D_MODEL), lambda j: (0, 0)),
                  pl.BlockSpec((D_MODEL, ADA_TN), lambda j: (0, j)),
                  pl.BlockSpec((1, ADA_TN), lambda j: (0, j))],
        out_specs=pl.BlockSpec((SUBLANES, ADA_TN), lambda j: (0, j)),
        out_shape=jax.ShapeDtypeStruct((SUBLANES, n), f32),
        compiler_params=pltpu.CompilerParams(vmem_limit_bytes=40 * 1024 * 1024),
        name="ada",
    )(c_pad, w_ada, b_ada)


INPROJ_TM = 512
INPROJ_TN = 1024


def _inproj_kernel(x_ref, sh_ref, sc_ref, w_ref, o_ref, h_scr):
    @pl.when(pl.program_id(1) == 0)
    def _():
        h = x_ref[...] * (1.0 + sc_ref[0]) + sh_ref[0]
        h_scr[...] = h.astype(bf16)

    o_ref[...] = _dot(h_scr[...], w_ref[...])


def _inproj(x2, mod3, w_cat, seq):
    t = x2.shape[0]
    per_b = seq // INPROJ_TM
    return pl.pallas_call(
        _inproj_kernel,
        grid=(t // INPROJ_TM, D_PROJ // INPROJ_TN),
        in_specs=[pl.BlockSpec((INPROJ_TM, D_MODEL), lambda i, j: (i, 0)),
                  pl.BlockSpec((1, 1, D_MODEL), lambda i, j: (i // per_b, 0, 0)),
                  pl.BlockSpec((1, 1, D_MODEL), lambda i, j: (i // per_b, 0, 1)),
                  pl.BlockSpec((D_MODEL, INPROJ_TN), lambda i, j: (0, j))],
        out_specs=pl.BlockSpec((INPROJ_TM, INPROJ_TN), lambda i, j: (i, j)),
        out_shape=jax.ShapeDtypeStruct((t, D_PROJ), f32),
        scratch_shapes=[pltpu.VMEM((INPROJ_TM, D_MODEL), bf16)],
        compiler_params=pltpu.CompilerParams(
            dimension_semantics=("arbitrary", "arbitrary"),
            vmem_limit_bytes=40 * 1024 * 1024),
        name="inproj",
    )(x2, mod3, mod3, w_cat)


CONV_TAIL = SUBLANES


def _split3(a):
    hi = a.astype(bf16)
    r1 = a - hi.astype(f32)
    mid = r1.astype(bf16)
    lo = (r1 - mid.astype(f32)).astype(bf16)
    return hi, mid, lo


def _pair_select(lane_lo, even, odd):
    return jnp.where(lane_lo, even, odd)


def _mixer_kernel(sink_ref, z_ref, xs_ref, q_ref, bc_ref, kvdt_ref, cw_ref, cb_ref, dtb_ref, alog_ref,
                  dskip_ref, gssm_ref, gatt_ref, o_ref, ext_scr, st_scr, kp_scr, vp_scr):
    c = pl.program_id(1)
    L = CHUNK

    @pl.when(c == 0)
    def _():
        ext_scr[pl.ds(0, CONV_TAIL), :] = jnp.zeros((CONV_TAIL, D_CONV), f32)
        st_scr[...] = jnp.zeros(st_scr.shape, f32)
        kp_scr[...] = jnp.zeros(kp_scr.shape, f32)
        vp_scr[...] = jnp.zeros(vp_scr.shape, f32)

    ext_scr[pl.ds(CONV_TAIL, L), pl.ds(0, D_SSM)] = xs_ref[...]
    ext_scr[pl.ds(CONV_TAIL, L), pl.ds(D_SSM, D_BC)] = bc_ref[...]
    conv = jnp.zeros((L, D_CONV), f32) + cb_ref[...]
    for w in range(CONV_WIDTH):
        off = CONV_TAIL - (CONV_WIDTH - 1) + w
        conv = conv + ext_scr[pl.ds(off, L), :] * cw_ref[pl.ds(w, 1), :]
    tail = ext_scr[pl.ds(L, CONV_TAIL), :]
    ext_scr[pl.ds(0, CONV_TAIL), :] = tail
    xbc = _silu(conv)
    xs = xbc[:, :D_SSM]

    kvdt = kvdt_ref[...]
    k_cur = kvdt[:, 0:D_KV]
    v_cur = kvdt[:, D_KV:2 * D_KV]
    dt_raw = kvdt[:, 2 * D_KV:3 * D_KV]

    xdt = dt_raw + dtb_ref[...]
    dt = jnp.maximum(xdt, 0.0) + jnp.log1p(jnp.exp(-jnp.abs(xdt)))
    a = dt * (-jnp.exp(alog_ref[...]))
    row_i = lax.broadcasted_iota(i32, (L, L), 0)
    col_i = lax.broadcasted_iota(i32, (L, L), 1)
    causal = row_i >= col_i
    tri = jnp.where(causal, 1.0, 0.0).astype(bf16)
    a_hi, a_mid, a_lo = _split3(a)
    a_cum = _dot(tri, a_hi) + _dot(tri, a_mid) + _dot(tri, a_lo)
    a_cum_t = a_cum.T
    dt_t = dt.T
    a_last = a_cum[L - 1:L, :]
    e_cum = jnp.exp(a_cum)
    to_end = jnp.exp(a_last - a_cum) * dt
    cdec = jnp.exp(a_last)
    lane_lo = lax.broadcasted_iota(i32, (L, LANES), 1) < HEADDIM
    lane_lo_row = lax.broadcasted_iota(i32, (1, LANES), 1) < HEADDIM

    y_parts = []
    for g in range(SSM_GROUPS):
        bm = xbc[:, D_SSM + g * SSM_STATE:D_SSM + (g + 1) * SSM_STATE]
        cm = xbc[:, D_SSM + (SSM_GROUPS + g) * SSM_STATE:D_SSM + (SSM_GROUPS + g + 1) * SSM_STATE]
        bm16 = bm.astype(bf16)
        cm16 = cm.astype(bf16)
        cb = _dot_nt(cm16, bm16)
        s_prev = st_scr[g]
        y_off = _dot(cm16, s_prev.astype(bf16))
        xs_scaled = []
        cd_rows = []
        for p in range(4):
            pair = g * 4 + p
            x_pair = xs[:, pair * LANES:(pair + 1) * LANES]
            x16 = x_pair.astype(bf16)
            yy = []
            for half in range(2):
                hd = 2 * pair + half
                seg = a_cum[:, hd:hd + 1] - a_cum_t[hd:hd + 1, :]
                decay = jnp.exp(jnp.where(causal, seg, NEG_INF))
                wmat = cb * decay * dt_t[hd:hd + 1, :]
                yy.append(_dot(wmat.astype(bf16), x16))
            hd0, hd1 = 2 * pair, 2 * pair + 1
            y_diag = _pair_select(lane_lo, yy[0], yy[1])
            e_pair = _pair_select(lane_lo, e_cum[:, hd0:hd0 + 1], e_cum[:, hd1:hd1 + 1])
            y_parts.append(y_diag + y_off[:, p * LANES:(p + 1) * LANES] * e_pair)
            te_pair = _pair_select(lane_lo, to_end[:, hd0:hd0 + 1], to_end[:, hd1:hd1 + 1])
            xs_scaled.append((x_pair * te_pair).astype(bf16))
            cd_rows.append(_pair_select(lane_lo_row, cdec[:, hd0:hd0 + 1], cdec[:, hd1:hd1 + 1]))
        xs_sc = jnp.concatenate(xs_scaled, axis=1)
        s_new = _dot(bm.T.astype(bf16), xs_sc)
        st_scr[g] = jnp.concatenate(cd_rows, axis=1) * s_prev + s_new
    y = jnp.concatenate(y_parts, axis=1) + dskip_ref[...] * xs
    y_ssd = _rms_norm(y * _silu(z_ref[...]), gssm_ref[...])
    o_ref[:, pl.ds(0, D_SSM)] = y_ssd.astype(o_ref.dtype)

    kcat = jnp.concatenate([kp_scr[...], k_cur], axis=0)
    vcat = jnp.concatenate([vp_scr[...], v_cur], axis=0)
    kp_scr[...] = k_cur
    vp_scr[...] = v_cur
    kroll = pltpu.roll(kcat, HEADDIM, axis=1)
    vroll = pltpu.roll(vcat, HEADDIM, axis=1)
    lane_lo2 = lax.broadcasted_iota(i32, (2 * L, LANES), 1) < HEADDIM
    qi = lax.broadcasted_iota(i32, (L, 2 * L), 0)
    kj = lax.broadcasted_iota(i32, (L, 2 * L), 1)
    dist = qi + L - kj
    valid = (dist >= 0) & (dist < L) & ((c > 0) | (kj >= L))
    att_parts = []
    for kvh in range(ATT_KV_HEADS):
        if kvh == 0:
            kd = jnp.where(lane_lo2, kcat, kroll).astype(bf16)
            vd = jnp.where(lane_lo2, vcat, vroll).astype(bf16)
        else:
            kd = jnp.where(lane_lo2, kroll, kcat).astype(bf16)
            vd = jnp.where(lane_lo2, vroll, vcat).astype(bf16)
        for p in range(4):
            pair = kvh * 4 + p
            q_pair = q_ref[:, pl.ds(pair * LANES, LANES)] * (HEADDIM ** -0.5)
            oo = []
            for half in range(2):
                hd = 2 * pair + half
                keep = lane_lo if half == 0 else jnp.logical_not(lane_lo)
                qm = jnp.where(keep, q_pair, 0.0).astype(bf16)
                s = _dot_nt(qm, kd)
                s = jnp.where(valid, s, NEG_INF)
                sink = sink_ref[hd]
                m = jnp.maximum(jnp.max(s, axis=1, keepdims=True), sink)
                pr = jnp.exp(s - m)
                denom = jnp.sum(pr, axis=1, keepdims=True) + jnp.exp(sink - m)
                oo.append(_dot((pr / denom).astype(bf16), vd))
            att_parts.append(_pair_select(lane_lo, oo[0], oo[1]))
    att = jnp.concatenate(att_parts, axis=1)
    o_ref[:, pl.ds(D_SSM, D_ATT)] = _rms_norm(att, gatt_ref[...]).astype(o_ref.dtype)


def _mixer(proj, sinks, conv_w8, conv_b, dtb, alog, dskip, gssm, gatt, batch, seq):
    t = proj.shape[0]
    nc = seq // CHUNK

    def tok(col):
        return lambda b, c: (b * nc + c, col)

    const = lambda b, c: (0, 0)
    return pl.pallas_call(
        _mixer_kernel,
        grid=(batch, nc),
        in_specs=[pl.BlockSpec(memory_space=pltpu.SMEM),
                  pl.BlockSpec((CHUNK, D_SSM), tok(0)),
                  pl.BlockSpec((CHUNK, D_SSM), tok(1)),
                  pl.BlockSpec((CHUNK, D_ATT), tok(2)),
                  pl.BlockSpec((CHUNK, D_BC), tok(6)),
                  pl.BlockSpec((CHUNK, D_BC), tok(7)),
                  pl.BlockSpec((SUBLANES, D_CONV), const),
                  pl.BlockSpec((1, D_CONV), const),
                  pl.BlockSpec((1, LANES), const),
                  pl.BlockSpec((1, LANES), const),
                  pl.BlockSpec((1, D_SSM), const),
                  pl.BlockSpec((1, D_SSM), const),
                  pl.BlockSpec((1, D_ATT), const)],
        out_specs=pl.BlockSpec((CHUNK, D_MODEL), lambda b, c: (b * nc + c, 0)),
        out_shape=jax.ShapeDtypeStruct((t, D_MODEL), bf16),
        scratch_shapes=[pltpu.VMEM((CONV_TAIL + CHUNK, D_CONV), f32),
                        pltpu.VMEM((SSM_GROUPS, SSM_STATE, D_SSM // SSM_GROUPS), f32),
                        pltpu.VMEM((CHUNK, D_KV), f32),
                        pltpu.VMEM((CHUNK, D_KV), f32)],
        compiler_params=pltpu.CompilerParams(
            dimension_semantics=("arbitrary", "arbitrary"),
            vmem_limit_bytes=40 * 1024 * 1024),
        name="mixer",
    )(sinks, proj, proj, proj, proj, proj, conv_w8, conv_b, dtb, alog, dskip, gssm, gatt)


OUT_TM = 256


def _outproj_kernel(y_ref, w_ref, x_ref, gate_ref, g_ref, b_ref, sh_ref, sc_ref, x1_ref, h2_ref):
    y = _dot(y_ref[...], w_ref[...])
    x1 = _layer_norm(ALPHA * x_ref[...] + gate_ref[0] * y, g_ref[...], b_ref[...])
    x1_ref[...] = x1
    h2_ref[...] = x1 * (1.0 + sc_ref[0]) + sh_ref[0]


def _outproj(ycat, w_out, x2, mod3, ln_g, ln_b, seq):
    t = x2.shape[0]
    per_b = seq // OUT_TM
    tok = lambda i: (i, 0)
    const = lambda i: (0, 0)

    def modspec(k):
        return pl.BlockSpec((1, 1, D_MODEL), lambda i: (i // per_b, 0, k))

    return pl.pallas_call(
        _outproj_kernel,
        grid=(t // OUT_TM,),
        in_specs=[pl.BlockSpec((OUT_TM, D_MODEL), tok),
                  pl.BlockSpec((D_MODEL, D_MODEL), const),
                  pl.BlockSpec((OUT_TM, D_MODEL), tok),
                  modspec(2), pl.BlockSpec((1, D_MODEL), const), pl.BlockSpec((1, D_MODEL), const),
                  modspec(3), modspec(4)],
        out_specs=[pl.BlockSpec((OUT_TM, D_MODEL), tok), pl.BlockSpec((OUT_TM, D_MODEL), tok)],
        out_shape=[jax.ShapeDtypeStruct((t, D_MODEL), f32), jax.ShapeDtypeStruct((t, D_MODEL), f32)],
        compiler_params=pltpu.CompilerParams(vmem_limit_bytes=48 * 1024 * 1024),
        name="outproj",
    )(ycat, w_out, x2, mod3, ln_g, ln_b, mod3, mod3)


ROUTE_TM = 256


ORDER_LAST = 2 ** 30


def _topk_rows(s, k, order, payload=None):
    vals, outs = [], []
    for _ in range(k):
        m = jnp.max(s, axis=0, keepdims=True)
        am = jnp.min(jnp.where(s == m, order, ORDER_LAST), axis=0, keepdims=True)
        hit = order == am
        vals.append(m)
        if payload is None:
            outs.append(am)
        else:
            outs.append(jnp.max(jnp.where(hit, payload, -1), axis=0, keepdims=True))
        s = jnp.where(hit, NEG_INF, s)
    return jnp.concatenate(vals, axis=0), jnp.concatenate(outs, axis=0)


def _candidate_slabs(v1, i1, v2, i2):
    tm = v1.shape[1]
    cands, orders, experts = [], [], []

    def slab(rows, vals, flat, eid, live):
        cands.append(jnp.where(live, vals, NEG_INF))
        orders.append(jnp.where(live, flat, ORDER_LAST))
        experts.append(eid)

    for k1, rows, n_live in ((0, 16, 16), (1, 8, 8), (2, 8, 5), (3, 8, 4)):
        k2 = lax.broadcasted_iota(i32, (rows, tm), 0)
        slab(rows, v1[k1:k1 + 1] + v2[0:rows], k1 * PEER_TOPK + k2,
             i1[k1:k1 + 1] * PEER_KEYS + i2[0:rows], k2 < n_live)
    for k2, rows, hi in ((0, 16, 16), (1, 8, 8), (2, 8, 5)):
        k1 = lax.broadcasted_iota(i32, (rows, tm), 0)
        slab(rows, v1[0:rows] + v2[k2:k2 + 1], k1 * PEER_TOPK + k2,
             i1[0:rows] * PEER_KEYS + i2[k2:k2 + 1], (k1 >= 4) & (k1 < hi))
    return jnp.concatenate(cands, axis=0), jnp.concatenate(orders, axis=0), jnp.concatenate(experts, axis=0)


def _route_kernel(h_ref, wq_ref, keys_ref, e_ref, g_ref):
    tm = h_ref.shape[0]
    q = _dot(h_ref[...].astype(bf16), wq_ref[...])
    key_id = lax.broadcasted_iota(i32, (PEER_KEYS, tm), 0)
    for h in range(PEER_HEADS):
        vals, idxs = [], []
        for cpart in range(2):
            col = (h * 2 + cpart) * PEER_HALF
            qhc = q[:, col:col + PEER_HALF].astype(bf16)
            sc = _dot_nt(keys_ref[h, cpart], qhc)
            v, ix = _topk_rows(sc, PEER_TOPK, key_id)
            vals.append(v)
            idxs.append(ix)
        cand, order, cidx = _candidate_slabs(vals[0], idxs[0], vals[1], idxs[1])
        best, expert = _topk_rows(cand, PEER_TOPK, order, payload=cidx)
        ex = jnp.exp(best - jnp.max(best, axis=0, keepdims=True))
        gate = ex / jnp.sum(ex, axis=0, keepdims=True)
        e_ref[pl.ds(h * PEER_TOPK, PEER_TOPK), :] = expert
        g_ref[pl.ds(h * PEER_TOPK, PEER_TOPK), :] = gate


def _route(h2, w_q, keys):
    t = h2.shape[0]
    return pl.pallas_call(
        _route_kernel,
        grid=(t // ROUTE_TM,),
        in_specs=[pl.BlockSpec((ROUTE_TM, D_MODEL), lambda i: (i, 0)),
                  pl.BlockSpec((D_MODEL, D_MODEL), lambda i: (0, 0)),
                  pl.BlockSpec((PEER_HEADS, 2, PEER_KEYS, PEER_HALF), lambda i: (0, 0, 0, 0))],
        out_specs=[pl.BlockSpec((PEER_SEL, ROUTE_TM), lambda i: (0, i)),
                   pl.BlockSpec((PEER_SEL, ROUTE_TM), lambda i: (0, i))],
        out_shape=[jax.ShapeDtypeStruct((PEER_SEL, t), i32), jax.ShapeDtypeStruct((PEER_SEL, t), f32)],
        compiler_params=pltpu.CompilerParams(vmem_limit_bytes=48 * 1024 * 1024),
        name="route",
    )(h2, w_q, keys)


EXP_TB = 128
EXP_SUB = 4
EXP_NSUB = EXP_TB // EXP_SUB
EXP_ROWS = EXP_SUB * PEER_SEL
D_UV = 2 * D_MODEL


def _experts_kernel(idx_ref, gs_ref, h_ref, x1_ref, gate_ref, g_ref, b_ref, uv_hbm, o_ref,
                    buf, y_scr, sem):
    def issue(j, slot):
        def body(tk, carry):
            for k in range(PEER_SEL):
                e = idx_ref[j * EXP_SUB + tk, k]
                pltpu.make_async_copy(uv_hbm.at[pl.ds(e, 1)], buf.at[slot, pl.ds(tk * PEER_SEL + k, 1)],
                                      sem.at[slot]).start()
            return carry
        lax.fori_loop(0, EXP_SUB, body, 0)

    def wait(slot):
        pltpu.make_async_copy(uv_hbm.at[pl.ds(0, EXP_ROWS)], buf.at[slot], sem.at[slot]).wait()

    issue(0, 0)

    def sub_block(j, carry):
        slot = j % 2

        @pl.when(j + 1 < EXP_NSUB)
        def _():
            issue(j + 1, 1 - slot)

        wait(slot)
        gs = gs_ref[j]
        for tk in range(EXP_SUB):
            rows = pl.ds(tk * PEER_SEL, PEER_SEL)
            tok_row = pl.ds(j * EXP_SUB + tk, 1)
            a = jnp.sum(buf[slot, rows, pl.ds(0, D_MODEL)] * h_ref[tok_row, :], axis=1, keepdims=True)
            wgt = gs[:, tk:tk + 1] * _gelu(a)
            y_scr[tok_row, :] = jnp.sum(buf[slot, rows, pl.ds(D_MODEL, D_MODEL)] * wgt, axis=0, keepdims=True)
        return carry

    lax.fori_loop(0, EXP_NSUB, sub_block, 0)
    o_ref[...] = _layer_norm(ALPHA * x1_ref[...] + gate_ref[0] * y_scr[...], g_ref[...], b_ref[...])


def _experts(idx, gslab, h2, x1, mod3, ln_g, ln_b, uv_tab, seq):
    t = h2.shape[0]
    per_b = seq // EXP_TB
    tok = lambda i: (i, 0)
    const = lambda i: (0, 0)
    return pl.pallas_call(
        _experts_kernel,
        grid=(t // EXP_TB,),
        in_specs=[pl.BlockSpec((EXP_TB, PEER_SEL), tok, memory_space=pltpu.SMEM),
                  pl.BlockSpec((EXP_NSUB, PEER_SEL, EXP_SUB), lambda i: (i, 0, 0)),
                  pl.BlockSpec((EXP_TB, D_MODEL), tok),
                  pl.BlockSpec((EXP_TB, D_MODEL), tok),
                  pl.BlockSpec((1, 1, D_MODEL), lambda i: (i // per_b, 0, 5)),
                  pl.BlockSpec((1, D_MODEL), const),
                  pl.BlockSpec((1, D_MODEL), const),
                  pl.BlockSpec(memory_space=pl.ANY)],
        out_specs=pl.BlockSpec((EXP_TB, D_MODEL), tok),
        out_shape=jax.ShapeDtypeStruct((t, D_MODEL), f32),
        scratch_shapes=[pltpu.VMEM((2, EXP_ROWS, D_UV), f32),
                        pltpu.VMEM((EXP_TB, D_MODEL), f32),
                        pltpu.SemaphoreType.DMA((2,))],
        compiler_params=pltpu.CompilerParams(vmem_limit_bytes=48 * 1024 * 1024),
        name="experts",
    )(idx, gslab, h2, x1, mod3, ln_g, ln_b, uv_tab)


def _regroup_w_in(w):
    o_xbc = D_SSM
    o_dt = o_xbc + D_CONV
    o_q = o_dt + SSM_HEADS
    o_k = o_q + D_ATT
    o_v = o_k + D_KV
    pad = jnp.zeros((w.shape[0], D_PROJ - (2 * D_SSM + D_ATT + D_BC + 2 * D_KV + SSM_HEADS)), w.dtype)
    return jnp.concatenate([w[:, :D_SSM], w[:, o_xbc:o_xbc + D_SSM], w[:, o_q:o_k],
                            w[:, o_xbc + D_SSM:o_dt], w[:, o_k:o_v], w[:, o_v:o_v + D_KV],
                            w[:, o_dt:o_q], pad], axis=1)


def _pad_lanes(v):
    return jnp.pad(v, (0, LANES - v.shape[0]))[None, :]


def kernel(x, c, w_ada, b_ada, w_in, conv_w, conv_b, dt_bias, a_log, d_skip, ssm_norm_g, attn_sinks, attn_norm_g, w_out, ln1_g, ln1_b, peer_w_q, peer_sub_keys, peer_u, peer_v, ln2_g, ln2_b):
    batch, seq, d = x.shape
    t = batch * seq
    x2 = x.reshape(t, d)
    for l in range(DEPTH):
        c_pad = jnp.pad(c, ((0, SUBLANES - batch), (0, 0)))
        mod = _ada(c_pad, w_ada[l], b_ada[l][None, :])
        mod3 = mod[:batch].reshape(batch, 1, N_MOD * d)
        proj = _inproj(x2, mod3, _regroup_w_in(w_in[l]).astype(bf16), seq)
        conv_w8 = jnp.pad(conv_w[l], ((0, SUBLANES - CONV_WIDTH), (0, 0)))
        ycat = _mixer(proj, attn_sinks[l], conv_w8, conv_b[l][None, :], _pad_lanes(dt_bias[l]),
                      _pad_lanes(a_log[l]), jnp.repeat(d_skip[l], HEADDIM)[None, :],
                      ssm_norm_g[l][None, :], attn_norm_g[l][None, :], batch, seq)
        x1, h2 = _outproj(ycat, w_out[l].astype(bf16), x2, mod3, ln1_g[l][None, :], ln1_b[l][None, :], seq)
        expert_t, gate_t = _route(h2, peer_w_q[l].astype(bf16), peer_sub_keys[l].astype(bf16))
        idx = expert_t.T
        gslab = gate_t.reshape(PEER_SEL, t // EXP_SUB, EXP_SUB).transpose(1, 0, 2)
        uv_tab = jnp.concatenate([peer_u[l], peer_v[l]], axis=1)
        x2 = _experts(idx, gslab, h2, x1, mod3, ln2_g[l][None, :], ln2_b[l][None, :], uv_tab, seq)
    return x2.reshape(batch, seq, d)
```

```python
import functools

import jax
import jax.numpy as jnp
from jax import lax
from jax.experimental import pallas as pl
from jax.experimental.pallas import tpu as pltpu

f32 = jnp.float32
bf16 = jnp.bfloat16
i32 = jnp.int32
u32 = jnp.uint32

D_MODEL = 2048
D_SSM = 1024
D_ATT = 1024
SSM_HEADS = 16
SSM_GROUPS = 2
SSM_STATE = 128
HEADDIM = 64
CONV_WIDTH = 4
CHUNK = 128
D_BC = 2 * SSM_GROUPS * SSM_STATE
D_CONV = D_SSM + D_BC
ATT_HEADS = 16
ATT_KV_HEADS = 2
D_KV = ATT_KV_HEADS * HEADDIM
PEER_HEADS = 8
PEER_KEYS = 128
PEER_TOPK = 16
PEER_HALF = 128
PEER_SEL = PEER_HEADS * PEER_TOPK
N_MOD = 6
DEPTH = 1
ALPHA = (2.0 * DEPTH) ** 0.25
EPS = 1e-5

LANES = 128
SUBLANES = 8
D_PROJ = 4096
HALF_D = D_MODEL // 2
NEG_INF = float("-inf")


def _silu(x):
    return x * jax.nn.sigmoid(x)


def _gelu(x):
    return 0.5 * x * (1.0 + lax.erf(x * (2.0 ** -0.5)))


def _dot(a, b):
    return jnp.dot(a, b, preferred_element_type=f32)


def _dot_nt(a, b):
    return lax.dot_general(a, b, (((1,), (1,)), ((), ())), preferred_element_type=f32)


def _layer_norm(x, g, b):
    mu = jnp.mean(x, axis=-1, keepdims=True)
    xc = x - mu
    var = jnp.mean(xc * xc, axis=-1, keepdims=True)
    return xc * lax.rsqrt(var + EPS) * g + b


def _rms_norm(x, g):
    return x * lax.rsqrt(jnp.mean(x * x, axis=-1, keepdims=True) + EPS) * g


ADA_TN = 1024


def _ada_kernel(c_ref, w_ref, b_ref, o_ref):
    sc = _silu(c_ref[...])
    o_ref[...] = _dot(sc.astype(bf16), w_ref[...].astype(bf16)) + b_ref[...]


def _ada(c_pad, w_ada, b_ada):
    n = w_ada.shape[1]
    return pl.pallas_call(
        _ada_kernel,
        grid=(n // ADA_TN,),
        in_specs=[pl.BlockSpec((SUBLANES, D_MODEL), lambda j: (0, 0)),
                  pl.BlockSpec((D_MODEL, ADA_TN), lambda j: (0, j)),
                  pl.BlockSpec((1, ADA_TN), lambda j: (0, j))],
        out_specs=pl.BlockSpec((SUBLANES, ADA_TN), lambda j: (0, j)),
        out_shape=jax.ShapeDtypeStruct((SUBLANES, n), f32),
        compiler_params=pltpu.CompilerParams(vmem_limit_bytes=40 * 1024 * 1024),
        name="ada",
    )(c_pad, w_ada, b_ada)


INPROJ_TM = 512
INPROJ_TN = 1024


def _inproj_kernel(x_ref, sh_ref, sc_ref, w_ref, o_ref, h_scr):
    @pl.when(pl.program_id(1) == 0)
    def _():
        h = x_ref[...] * (1.0 + sc_ref[0]) + sh_ref[0]
        h_scr[...] = h.astype(bf16)

    o_ref[...] = _dot(h_scr[...], w_ref[...])


def _inproj(x2, mod3, w_cat, seq):
    t = x2.shape[0]
    per_b = seq // INPROJ_TM
    return pl.pallas_call(
        _inproj_kernel,
        grid=(t // INPROJ_TM, D_PROJ // INPROJ_TN),
        in_specs=[pl.BlockSpec((INPROJ_TM, D_MODEL), lambda i, j: (i, 0)),
                  pl.BlockSpec((1, 1, D_MODEL), lambda i, j: (i // per_b, 0, 0)),
                  pl.BlockSpec((1, 1, D_MODEL), lambda i, j: (i // per_b, 0, 1)),
                  pl.BlockSpec((D_MODEL, INPROJ_TN), lambda i, j: (0, j))],
        out_specs=pl.BlockSpec((INPROJ_TM, INPROJ_TN), lambda i, j: (i, j)),
        out_shape=jax.ShapeDtypeStruct((t, D_PROJ), f32),
        scratch_shapes=[pltpu.VMEM((INPROJ_TM, D_MODEL), bf16)],
        compiler_params=pltpu.CompilerParams(
            dimension_semantics=("arbitrary", "arbitrary"),
            vmem_limit_bytes=40 * 1024 * 1024),
        name="inproj",
    )(x2, mod3, mod3, w_cat)


CONV_TAIL = SUBLANES


def _split3(a):
    hi = a.astype(bf16)
    r1 = a - hi.astype(f32)
    mid = r1.astype(bf16)
    lo = (r1 - mid.astype(f32)).astype(bf16)
    return hi, mid, lo


def _pair_select(lane_lo, even, odd):
    return jnp.where(lane_lo, even, odd)


def _mixer_kernel(sink_ref, z_ref, xs_ref, q_ref, bc_ref, kvdt_ref, cw_ref, cb_ref, dtb_ref, alog_ref,
                  dskip_ref, gssm_ref, gatt_ref, o_ref, ext_scr, st_scr, kp_scr, vp_scr):
    c = pl.program_id(1)
    L = CHUNK

    @pl.when(c == 0)
    def _():
        ext_scr[pl.ds(0, CONV_TAIL), :] = jnp.zeros((CONV_TAIL, D_CONV), f32)
        st_scr[...] = jnp.zeros(st_scr.shape, f32)
        kp_scr[...] = jnp.zeros(kp_scr.shape, f32)
        vp_scr[...] = jnp.zeros(vp_scr.shape, f32)

    ext_scr[pl.ds(CONV_TAIL, L), pl.ds(0, D_SSM)] = xs_ref[...]
    ext_scr[pl.ds(CONV_TAIL, L), pl.ds(D_SSM, D_BC)] = bc_ref[...]
    conv = jnp.zeros((L, D_CONV), f32) + cb_ref[...]
    for w in range(CONV_WIDTH):
        off = CONV_TAIL - (CONV_WIDTH - 1) + w
        conv = conv + ext_scr[pl.ds(off, L), :] * cw_ref[pl.ds(w, 1), :]
    tail = ext_scr[pl.ds(L, CONV_TAIL), :]
    ext_scr[pl.ds(0, CONV_TAIL), :] = tail
    xbc = _silu(conv)
    xs = xbc[:, :D_SSM]

    kvdt = kvdt_ref[...]
    k_cur = kvdt[:, 0:D_KV]
    v_cur = kvdt[:, D_KV:2 * D_KV]
    dt_raw = kvdt[:, 2 * D_KV:3 * D_KV]

    xdt = dt_raw + dtb_ref[...]
    dt = jnp.maximum(xdt, 0.0) + jnp.log1p(jnp.exp(-jnp.abs(xdt)))
    a = dt * (-jnp.exp(alog_ref[...]))
    row_i = lax.broadcasted_iota(i32, (L, L), 0)
    col_i = lax.broadcasted_iota(i32, (L, L), 1)
    causal = row_i >= col_i
    tri = jnp.where(causal, 1.0, 0.0).astype(bf16)
    a_hi, a_mid, a_lo = _split3(a)
    a_cum = _dot(tri, a_hi) + _dot(tri, a_mid) + _dot(tri, a_lo)
    a_cum_t = a_cum.T
    dt_t = dt.T
    a_last = a_cum[L - 1:L, :]
    e_cum = jnp.exp(a_cum)
    to_end = jnp.exp(a_last - a_cum) * dt
    cdec = jnp.exp(a_last)
    lane_lo = lax.broadcasted_iota(i32, (L, LANES), 1) < HEADDIM
    lane_lo_row = lax.broadcasted_iota(i32, (1, LANES), 1) < HEADDIM

    y_parts = []
    for g in range(SSM_GROUPS):
        bm = xbc[:, D_SSM + g * SSM_STATE:D_SSM + (g + 1) * SSM_STATE]
        cm = xbc[:, D_SSM + (SSM_GROUPS + g) * SSM_STATE:D_SSM + (SSM_GROUPS + g + 1) * SSM_STATE]
        bm16 = bm.astype(bf16)
        cm16 = cm.astype(bf16)
        cb = _dot_nt(cm16, bm16)
        s_prev = st_scr[g]
        y_off = _dot(cm16, s_prev.astype(bf16))
        xs_scaled = []
        cd_rows = []
        for p in range(4):
            pair = g * 4 + p
            x_pair = xs[:, pair * LANES:(pair + 1) * LANES]
            x16 = x_pair.astype(bf16)
            yy = []
            for half in range(2):
                hd = 2 * pair + half
                seg = a_cum[:, hd:hd + 1] - a_cum_t[hd:hd + 1, :]
                decay = jnp.exp(jnp.where(causal, seg, NEG_INF))
                wmat = cb * decay * dt_t[hd:hd + 1, :]
                yy.append(_dot(wmat.astype(bf16), x16))
            hd0, hd1 = 2 * pair, 2 * pair + 1
            y_diag = _pair_select(lane_lo, yy[0], yy[1])
            e_pair = _pair_select(lane_lo, e_cum[:, hd0:hd0 + 1], e_cum[:, hd1:hd1 + 1])
            y_parts.append(y_diag + y_off[:, p * LANES:(p + 1) * LANES] * e_pair)
            te_pair = _pair_select(lane_lo, to_end[:, hd0:hd0 + 1], to_end[:, hd1:hd1 + 1])
            xs_scaled.append((x_pair * te_pair).astype(bf16))
            cd_rows.append(_pair_select(lane_lo_row, cdec[:, hd0:hd0 + 1], cdec[:, hd1:hd1 + 1]))
        xs_sc = jnp.concatenate(xs_scaled, axis=1)
        s_new = _dot(bm.T.astype(bf16), xs_sc)
        st_scr[g] = jnp.concatenate(cd_rows, axis=1) * s_prev + s_new
    y = jnp.concatenate(y_parts, axis=1) + dskip_ref[...] * xs
    y_ssd = _rms_norm(y * _silu(z_ref[...]), gssm_ref[...])
    o_ref[:, pl.ds(0, D_SSM)] = y_ssd.astype(o_ref.dtype)

    kcat = jnp.concatenate([kp_scr[...], k_cur], axis=0)
    vcat = jnp.concatenate([vp_scr[...], v_cur], axis=0)
    kp_scr[...] = k_cur
    vp_scr[...] = v_cur
    kroll = pltpu.roll(kcat, HEADDIM, axis=1)
    vroll = pltpu.roll(vcat, HEADDIM, axis=1)
    lane_lo2 = lax.broadcasted_iota(i32, (2 * L, LANES), 1) < HEADDIM
    qi = lax.broadcasted_iota(i32, (L, 2 * L), 0)
    kj = lax.broadcasted_iota(i32, (L, 2 * L), 1)
    dist = qi + L - kj
    valid = (dist >= 0) & (dist < L) & ((c > 0) | (kj >= L))
    att_parts = []
    for kvh in range(ATT_KV_HEADS):
        if kvh == 0:
            kd = jnp.where(lane_lo2, kcat, kroll).astype(bf16)
            vd = jnp.where(lane_lo2, vcat, vroll).astype(bf16)
        else:
            kd = jnp.where(lane_lo2, kroll, kcat).astype(bf16)
            vd = jnp.where(lane_lo2, vroll, vcat).astype(bf16)
        for p in range(4):
            pair = kvh * 4 + p
            q_pair = q_ref[:, pl.ds(pair * LANES, LANES)] * (HEADDIM ** -0.5)
            oo = []
            for half in range(2):
                hd = 2 * pair + half
                keep = lane_lo if half == 0 else jnp.logical_not(lane_lo)
                qm = jnp.where(keep, q_pair, 0.0).astype(bf16)
                s = _dot_nt(qm, kd)
                s = jnp.where(valid, s, NEG_INF)
                sink = sink_ref[hd]
                m = jnp.maximum(jnp.max(s, axis=1, keepdims=True), sink)
                pr = jnp.exp(s - m)
                denom = jnp.sum(pr, axis=1, keepdims=True) + jnp.exp(sink - m)
                oo.append(_dot((pr / denom).astype(bf16), vd))
            att_parts.append(_pair_select(lane_lo, oo[0], oo[1]))
    att = jnp.concatenate(att_parts, axis=1)
    o_ref[:, pl.ds(D_SSM, D_ATT)] = _rms_norm(att, gatt_ref[...]).astype(o_ref.dtype)


def _mixer(proj, sinks, conv_w8, conv_b, dtb, alog, dskip, gssm, gatt, batch, seq):
    t = proj.shape[0]
    nc = seq // CHUNK

    def tok(col):
        return lambda b, c: (b * nc + c, col)

    const = lambda b, c: (0, 0)
    return pl.pallas_call(
        _mixer_kernel,
        grid=(batch, nc),
        in_specs=[pl.BlockSpec(memory_space=pltpu.SMEM),
                  pl.BlockSpec((CHUNK, D_SSM), tok(0)),
                  pl.BlockSpec((CHUNK, D_SSM), tok(1)),
                  pl.BlockSpec((CHUNK, D_ATT), tok(2)),
                  pl.BlockSpec((CHUNK, D_BC), tok(6)),
                  pl.BlockSpec((CHUNK, D_BC), tok(7)),
                  pl.BlockSpec((SUBLANES, D_CONV), const),
                  pl.BlockSpec((1, D_CONV), const),
                  pl.BlockSpec((1, LANES), const),
                  pl.BlockSpec((1, LANES), const),
                  pl.BlockSpec((1, D_SSM), const),
                  pl.BlockSpec((1, D_SSM), const),
                  pl.BlockSpec((1, D_ATT), const)],
        out_specs=pl.BlockSpec((CHUNK, D_MODEL), lambda b, c: (b * nc + c, 0)),
        out_shape=jax.ShapeDtypeStruct((t, D_MODEL), bf16),
        scratch_shapes=[pltpu.VMEM((CONV_TAIL + CHUNK, D_CONV), f32),
                        pltpu.VMEM((SSM_GROUPS, SSM_STATE, D_SSM // SSM_GROUPS), f32),
                        pltpu.VMEM((CHUNK, D_KV), f32),
                        pltpu.VMEM((CHUNK, D_KV), f32)],
        compiler_params=pltpu.CompilerParams(
            dimension_semantics=("arbitrary", "arbitrary"),
            vmem_limit_bytes=40 * 1024 * 1024),
        name="mixer",
    )(sinks, proj, proj, proj, proj, proj, conv_w8, conv_b, dtb, alog, dskip, gssm, gatt)


OUT_TM = 256


def _outproj_kernel(y_ref, w_ref, x_ref, gate_ref, g_ref, b_ref, sh_ref, sc_ref, x1_ref, h2_ref):
    y = _dot(y_ref[...], w_ref[...])
    x1 = _layer_norm(ALPHA * x_ref[...] + gate_ref[0] * y, g_ref[...], b_ref[...])
    x1_ref[...] = x1
    h2_ref[...] = x1 * (1.0 + sc_ref[0]) + sh_ref[0]


def _outproj(ycat, w_out, x2, mod3, ln_g, ln_b, seq):
    t = x2.shape[0]
    per_b = seq // OUT_TM
    tok = lambda i: (i, 0)
    const = lambda i: (0, 0)

    def modspec(k):
        return pl.BlockSpec((1, 1, D_MODEL), lambda i: (i // per_b, 0, k))

    return pl.pallas_call(
        _outproj_kernel,
        grid=(t // OUT_TM,),
        in_specs=[pl.BlockSpec((OUT_TM, D_MODEL), tok),
                  pl.BlockSpec((D_MODEL, D_MODEL), const),
                  pl.BlockSpec((OUT_TM, D_MODEL), tok),
                  modspec(2), pl.BlockSpec((1, D_MODEL), const), pl.BlockSpec((1, D_MODEL), const),
                  modspec(3), modspec(4)],
        out_specs=[pl.BlockSpec((OUT_TM, D_MODEL), tok), pl.BlockSpec((OUT_TM, D_MODEL), tok)],
        out_shape=[jax.ShapeDtypeStruct((t, D_MODEL), f32), jax.ShapeDtypeStruct((t, D_MODEL), f32)],
        compiler_params=pltpu.CompilerParams(vmem_limit_bytes=48 * 1024 * 1024),
        name="outproj",
    )(ycat, w_out, x2, mod3, ln_g, ln_b, mod3, mod3)


ROUTE_TM = 256


ORDER_LAST = 2 ** 30


def _topk_rows(s, k, order, payload=None):
    vals, outs = [], []
    for _ in range(k):
        m = jnp.max(s, axis=0, keepdims=True)
        am = jnp.min(jnp.where(s == m, order, ORDER_LAST), axis=0, keepdims=True)
        hit = order == am
        vals.append(m)
        if payload is None:
            outs.append(am)
        else:
            outs.append(jnp.max(jnp.where(hit, payload, -1), axis=0, keepdims=True))
        s = jnp.where(hit, NEG_INF, s)
    return jnp.concatenate(vals, axis=0), jnp.concatenate(outs, axis=0)


def _candidate_slabs(v1, i1, v2, i2):
    tm = v1.shape[1]
    cands, orders, experts = [], [], []

    def slab(rows, vals, flat, eid, live):
        cands.append(jnp.where(live, vals, NEG_INF))
        orders.append(jnp.where(live, flat, ORDER_LAST))
        experts.append(eid)

    for k1, rows, n_live in ((0, 16, 16), (1, 8, 8), (2, 8, 5), (3, 8, 4)):
        k2 = lax.broadcasted_iota(i32, (rows, tm), 0)
        slab(rows, v1[k1:k1 + 1] + v2[0:rows], k1 * PEER_TOPK + k2,
             i1[k1:k1 + 1] * PEER_KEYS + i2[0:rows], k2 < n_live)
    for k2, rows, hi in ((0, 16, 16), (1, 8, 8), (2, 8, 5)):
        k1 = lax.broadcasted_iota(i32, (rows, tm), 0)
        slab(rows, v1[0:rows] + v2[k2:k2 + 1], k1 * PEER_TOPK + k2,
             i1[0:rows] * PEER_KEYS + i2[k2:k2 + 1], (k1 >= 4) & (k1 < hi))
    return jnp.concatenate(cands, axis=0), jnp.concatenate(orders, axis=0), jnp.concatenate(experts, axis=0)


def _route_kernel(h_ref, wq_ref, keys_ref, e_ref, g_ref):
    tm = h_ref.shape[0]
    q = _dot(h_ref[...].astype(bf16), wq_ref[...])
    key_id = lax.broadcasted_iota(i32, (PEER_KEYS, tm), 0)
    for h in range(PEER_HEADS):
        vals, idxs = [], []
        for cpart in range(2):
            col = (h * 2 + cpart) * PEER_HALF
            qhc = q[:, col:col + PEER_HALF].astype(bf16)
            sc = _dot_nt(keys_ref[h, cpart], qhc)
            v, ix = _topk_rows(sc, PEER_TOPK, key_id)
            vals.append(v)
            idxs.append(ix)
        cand, order, cidx = _candidate_slabs(vals[0], idxs[0], vals[1], idxs[1])
        best, expert = _topk_rows(cand, PEER_TOPK, order, payload=cidx)
        ex = jnp.exp(best - jnp.max(best, axis=0, keepdims=True))
        gate = ex / jnp.sum(ex, axis=0, keepdims=True)
        e_ref[pl.ds(h * PEER_TOPK, PEER_TOPK), :] = expert
        g_ref[pl.ds(h * PEER_TOPK, PEER_TOPK), :] = gate


def _route(h2, w_q, keys):
    t = h2.shape[0]
    return pl.pallas_call(
        _route_kernel,
        grid=(t // ROUTE_TM,),
        in_specs=[pl.BlockSpec((ROUTE_TM, D_MODEL), lambda i: (i, 0)),
                  pl.BlockSpec((D_MODEL, D_MODEL), lambda i: (0, 0)),
                  pl.BlockSpec((PEER_HEADS, 2, PEER_KEYS, PEER_HALF), lambda i: (0, 0, 0, 0))],
        out_specs=[pl.BlockSpec((PEER_SEL, ROUTE_TM), lambda i: (0, i)),
                   pl.BlockSpec((PEER_SEL, ROUTE_TM), lambda i: (0, i))],
        out_shape=[jax.ShapeDtypeStruct((PEER_SEL, t), i32), jax.ShapeDtypeStruct((PEER_SEL, t), f32)],
        compiler_params=pltpu.CompilerParams(vmem_limit_bytes=48 * 1024 * 1024),
        name="route",
    )(h2, w_q, keys)


EXP_TB = 128
EXP_SUB = 8
EXP_NSUB = EXP_TB // EXP_SUB
EXP_ROWS = EXP_SUB * PEER_SEL
D_UV = 2 * D_MODEL


def _experts_kernel(idx_ref, idx_next_ref, gt_ref, h_ref, x1_ref, gate_ref, g_ref, b_ref, uv_hbm, o_ref,
                    buf, y_scr, sem):
    step = pl.program_id(0)
    last_step = pl.num_programs(0) - 1

    def issue(ids_ref, j, slot):
        def body(tk, carry):
            for k in range(PEER_SEL):
                e = ids_ref[j * EXP_SUB + tk, k]
                pltpu.make_async_copy(uv_hbm.at[e], buf.at[slot, pl.ds(tk * PEER_SEL + k, 1)],
                                      sem.at[slot]).start()
            return carry
        lax.fori_loop(0, EXP_SUB, body, 0)

    def wait(slot):
        pltpu.make_async_copy(buf.at[slot], buf.at[slot], sem.at[slot]).wait()

    @pl.when(step == 0)
    def _():
        issue(idx_ref, 0, 0)

    def sub_block(j, carry):
        slot = j % 2

        @pl.when(j + 1 < EXP_NSUB)
        def _():
            issue(idx_ref, j + 1, 1 - slot)

        @pl.when((j + 1 == EXP_NSUB) & (step < last_step))
        def _():
            issue(idx_next_ref, 0, 1 - slot)

        wait(slot)
        gs = pltpu.roll(gt_ref[...], (EXP_TB - j * EXP_SUB) & (EXP_TB - 1), axis=1)
        for tk in range(EXP_SUB):
            rows = pl.ds(tk * PEER_SEL, PEER_SEL)
            tok_row = pl.ds(j * EXP_SUB + tk, 1)
            a = jnp.sum(buf[slot, rows, pl.ds(0, D_MODEL)] * h_ref[tok_row, :], axis=1, keepdims=True)
            wgt = gs[:, tk:tk + 1] * _gelu(a)
            y_scr[tok_row, :] = jnp.sum(buf[slot, rows, pl.ds(D_MODEL, D_MODEL)] * wgt, axis=0, keepdims=True)
        return carry

    lax.fori_loop(0, EXP_NSUB, sub_block, 0)
    o_ref[...] = _layer_norm(ALPHA * x1_ref[...] + gate_ref[0] * y_scr[...], g_ref[...], b_ref[...])


def _experts(idx, gate_t, h2, x1, mod3, ln_g, ln_b, uv_tab, seq):
    t = h2.shape[0]
    n_steps = t // EXP_TB
    per_b = seq // EXP_TB
    assert EXP_NSUB % 2 == 0, "the cross-step prefetch assumes every grid step starts on slot 0"
    tok = lambda i: (i, 0)
    const = lambda i: (0, 0)
    return pl.pallas_call(
        _experts_kernel,
        grid=(n_steps,),
        in_specs=[pl.BlockSpec((EXP_TB, PEER_SEL), tok, memory_space=pltpu.SMEM),
                  pl.BlockSpec((EXP_TB, PEER_SEL), lambda i: (jnp.minimum(i + 1, n_steps - 1), 0),
                               memory_space=pltpu.SMEM),
                  pl.BlockSpec((PEER_SEL, EXP_TB), lambda i: (0, i)),
                  pl.BlockSpec((EXP_TB, D_MODEL), tok),
                  pl.BlockSpec((EXP_TB, D_MODEL), tok),
                  pl.BlockSpec((1, 1, D_MODEL), lambda i: (i // per_b, 0, 5)),
                  pl.BlockSpec((1, D_MODEL), const),
                  pl.BlockSpec((1, D_MODEL), const),
                  pl.BlockSpec(memory_space=pl.ANY)],
        out_specs=pl.BlockSpec((EXP_TB, D_MODEL), tok),
        out_shape=jax.ShapeDtypeStruct((t, D_MODEL), f32),
        scratch_shapes=[pltpu.VMEM((2, EXP_ROWS, D_UV), f32),
                        pltpu.VMEM((EXP_TB, D_MODEL), f32),
                        pltpu.SemaphoreType.DMA((2,))],
        compiler_params=pltpu.CompilerParams(
            dimension_semantics=("arbitrary",),
            vmem_limit_bytes=56 * 1024 * 1024),
        name="experts",
    )(idx, idx, gate_t, h2, x1, mod3, ln_g, ln_b, uv_tab)


def _regroup_w_in(w):
    o_xbc = D_SSM
    o_dt = o_xbc + D_CONV
    o_q = o_dt + SSM_HEADS
    o_k = o_q + D_ATT
    o_v = o_k + D_KV
    pad = jnp.zeros((w.shape[0], D_PROJ - (2 * D_SSM + D_ATT + D_BC + 2 * D_KV + SSM_HEADS)), w.dtype)
    return jnp.concatenate([w[:, :D_SSM], w[:, o_xbc:o_xbc + D_SSM], w[:, o_q:o_k],
                            w[:, o_xbc + D_SSM:o_dt], w[:, o_k:o_v], w[:, o_v:o_v + D_KV],
                            w[:, o_dt:o_q], pad], axis=1)


def _pad_lanes(v):
    return jnp.pad(v, (0, LANES - v.shape[0]))[None, :]


def kernel(x, c, w_ada, b_ada, w_in, conv_w, conv_b, dt_bias, a_log, d_skip, ssm_norm_g, attn_sinks, attn_norm_g, w_out, ln1_g, ln1_b, peer_w_q, peer_sub_keys, peer_u, peer_v, ln2_g, ln2_b):
    batch, seq, d = x.shape
    t = batch * seq
    x2 = x.reshape(t, d)
    for l in range(DEPTH):
        c_pad = jnp.pad(c, ((0, SUBLANES - batch), (0, 0)))
        mod = _ada(c_pad, w_ada[l], b_ada[l][None, :])
        mod3 = mod[:batch].reshape(batch, 1, N_MOD * d)
        proj = _inproj(x2, mod3, _regroup_w_in(w_in[l]).astype(bf16), seq)
        conv_w8 = jnp.pad(conv_w[l], ((0, SUBLANES - CONV_WIDTH), (0, 0)))
        ycat = _mixer(proj, attn_sinks[l], conv_w8, conv_b[l][None, :], _pad_lanes(dt_bias[l]),
                      _pad_lanes(a_log[l]), jnp.repeat(d_skip[l], HEADDIM)[None, :],
                      ssm_norm_g[l][None, :], attn_norm_g[l][None, :], batch, seq)
        x1, h2 = _outproj(ycat, w_out[l].astype(bf16), x2, mod3, ln1_g[l][None, :], ln1_b[l][None, :], seq)
        expert_t, gate_t = _route(h2, peer_w_q[l].astype(bf16), peer_sub_keys[l].astype(bf16))
        idx = expert_t.T
        uv_tab = jnp.concatenate([peer_u[l][:, None, :], peer_v[l][:, None, :]], axis=2)
        x2 = _experts(idx, gate_t, h2, x1, mod3, ln2_g[l][None, :], ln2_b[l][None, :], uv_tab, seq)
    return x2.reshape(batch, seq, d)
```

```python
import functools

import jax
import jax.numpy as jnp
from jax import lax
from jax.experimental import pallas as pl
from jax.experimental.pallas import tpu as pltpu

f32 = jnp.float32
bf16 = jnp.bfloat16
i32 = jnp.int32
u32 = jnp.uint32

D_MODEL = 2048
D_SSM = 1024
D_ATT = 1024
SSM_HEADS = 16
SSM_GROUPS = 2
SSM_STATE = 128
HEADDIM = 64
CONV_WIDTH = 4
CHUNK = 128
D_BC = 2 * SSM_GROUPS * SSM_STATE
D_CONV = D_SSM + D_BC
ATT_HEADS = 16
ATT_KV_HEADS = 2
D_KV = ATT_KV_HEADS * HEADDIM
PEER_HEADS = 8
PEER_KEYS = 128
PEER_TOPK = 16
PEER_HALF = 128
PEER_SEL = PEER_HEADS * PEER_TOPK
N_MOD = 6
DEPTH = 1
ALPHA = (2.0 * DEPTH) ** 0.25
EPS = 1e-5

LANES = 128
SUBLANES = 8
D_PROJ = 4096
HALF_D = D_MODEL // 2
NEG_INF = float("-inf")


def _silu(x):
    return x * jax.nn.sigmoid(x)


def _gelu(x):
    return 0.5 * x * (1.0 + lax.erf(x * (2.0 ** -0.5)))


def _dot(a, b):
    return jnp.dot(a, b, preferred_element_type=f32)


def _dot_nt(a, b):
    return lax.dot_general(a, b, (((1,), (1,)), ((), ())), preferred_element_type=f32)


def _layer_norm(x, g, b):
    mu = jnp.mean(x, axis=-1, keepdims=True)
    xc = x - mu
    var = jnp.mean(xc * xc, axis=-1, keepdims=True)
    return xc * lax.rsqrt(var + EPS) * g + b


def _rms_norm(x, g):
    return x * lax.rsqrt(jnp.mean(x * x, axis=-1, keepdims=True) + EPS) * g


ADA_TN = 1024


def _ada_kernel(c_ref, w_ref, b_ref, o_ref):
    sc = _silu(c_ref[...])
    o_ref[...] = _dot(sc.astype(bf16), w_ref[...].astype(bf16)) + b_ref[...]


def _ada(c_pad, w_ada, b_ada):
    n = w_ada.shape[1]
    return pl.pallas_call(
        _ada_kernel,
        grid=(n // ADA_TN,),
        in_specs=[pl.BlockSpec((SUBLANES, D_MODEL), lambda j: (0, 0)),
                  pl.BlockSpec((D_MODEL, ADA_TN), lambda j: (0, j)),
                  pl.BlockSpec((1, ADA_TN), lambda j: (0, j))],
        out_specs=pl.BlockSpec((SUBLANES, ADA_TN), lambda j: (0, j)),
        out_shape=jax.ShapeDtypeStruct((SUBLANES, n), f32),
        compiler_params=pltpu.CompilerParams(vmem_limit_bytes=40 * 1024 * 1024),
        name="ada",
    )(c_pad, w_ada, b_ada)


INPROJ_TM = 512
INPROJ_TN = 1024


def _inproj_kernel(x_ref, sh_ref, sc_ref, w_ref, o_ref, h_scr):
    @pl.when(pl.program_id(1) == 0)
    def _():
        h = x_ref[...] * (1.0 + sc_ref[0]) + sh_ref[0]
        h_scr[...] = h.astype(bf16)

    o_ref[...] = _dot(h_scr[...], w_ref[...])


def _inproj(x2, mod3, w_cat, seq):
    t = x2.shape[0]
    per_b = seq // INPROJ_TM
    return pl.pallas_call(
        _inproj_kernel,
        grid=(t // INPROJ_TM, D_PROJ // INPROJ_TN),
        in_specs=[pl.BlockSpec((INPROJ_TM, D_MODEL), lambda i, j: (i, 0)),
                  pl.BlockSpec((1, 1, D_MODEL), lambda i, j: (i // per_b, 0, 0)),
                  pl.BlockSpec((1, 1, D_MODEL), lambda i, j: (i // per_b, 0, 1)),
                  pl.BlockSpec((D_MODEL, INPROJ_TN), lambda i, j: (0, j))],
        out_specs=pl.BlockSpec((INPROJ_TM, INPROJ_TN), lambda i, j: (i, j)),
        out_shape=jax.ShapeDtypeStruct((t, D_PROJ), f32),
        scratch_shapes=[pltpu.VMEM((INPROJ_TM, D_MODEL), bf16)],
        compiler_params=pltpu.CompilerParams(
            dimension_semantics=("arbitrary", "arbitrary"),
            vmem_limit_bytes=40 * 1024 * 1024),
        name="inproj",
    )(x2, mod3, mod3, w_cat)


CONV_TAIL = SUBLANES


def _split3(a):
    hi = a.astype(bf16)
    r1 = a - hi.astype(f32)
    mid = r1.astype(bf16)
    lo = (r1 - mid.astype(f32)).astype(bf16)
    return hi, mid, lo


def _pair_select(lane_lo, even, odd):
    return jnp.where(lane_lo, even, odd)


def _mixer_kernel(sink_ref, z_ref, xs_ref, q_ref, bc_ref, kvdt_ref, cw_ref, cb_ref, dtb_ref, alog_ref,
                  dskip_ref, gssm_ref, gatt_ref, o_ref, ext_scr, st_scr, kp_scr, vp_scr):
    c = pl.program_id(1)
    L = CHUNK

    @pl.when(c == 0)
    def _():
        ext_scr[pl.ds(0, CONV_TAIL), :] = jnp.zeros((CONV_TAIL, D_CONV), f32)
        st_scr[...] = jnp.zeros(st_scr.shape, f32)
        kp_scr[...] = jnp.zeros(kp_scr.shape, f32)
        vp_scr[...] = jnp.zeros(vp_scr.shape, f32)

    ext_scr[pl.ds(CONV_TAIL, L), pl.ds(0, D_SSM)] = xs_ref[...]
    ext_scr[pl.ds(CONV_TAIL, L), pl.ds(D_SSM, D_BC)] = bc_ref[...]
    conv = jnp.zeros((L, D_CONV), f32) + cb_ref[...]
    for w in range(CONV_WIDTH):
        off = CONV_TAIL - (CONV_WIDTH - 1) + w
        conv = conv + ext_scr[pl.ds(off, L), :] * cw_ref[pl.ds(w, 1), :]
    tail = ext_scr[pl.ds(L, CONV_TAIL), :]
    ext_scr[pl.ds(0, CONV_TAIL), :] = tail
    xbc = _silu(conv)
    xs = xbc[:, :D_SSM]

    kvdt = kvdt_ref[...]
    k_cur = kvdt[:, 0:D_KV]
    v_cur = kvdt[:, D_KV:2 * D_KV]
    dt_raw = kvdt[:, 2 * D_KV:3 * D_KV]

    xdt = dt_raw + dtb_ref[...]
    dt = jnp.maximum(xdt, 0.0) + jnp.log1p(jnp.exp(-jnp.abs(xdt)))
    a = dt * (-jnp.exp(alog_ref[...]))
    row_i = lax.broadcasted_iota(i32, (L, L), 0)
    col_i = lax.broadcasted_iota(i32, (L, L), 1)
    causal = row_i >= col_i
    tri = jnp.where(causal, 1.0, 0.0).astype(bf16)
    a_hi, a_mid, a_lo = _split3(a)
    a_cum = _dot(tri, a_hi) + _dot(tri, a_mid) + _dot(tri, a_lo)
    a_cum_t = a_cum.T
    dt_t = dt.T
    a_last = a_cum[L - 1:L, :]
    e_cum = jnp.exp(a_cum)
    to_end = jnp.exp(a_last - a_cum) * dt
    cdec = jnp.exp(a_last)
    lane_lo = lax.broadcasted_iota(i32, (L, LANES), 1) < HEADDIM
    lane_lo_row = lax.broadcasted_iota(i32, (1, LANES), 1) < HEADDIM

    y_parts = []
    for g in range(SSM_GROUPS):
        bm = xbc[:, D_SSM + g * SSM_STATE:D_SSM + (g + 1) * SSM_STATE]
        cm = xbc[:, D_SSM + (SSM_GROUPS + g) * SSM_STATE:D_SSM + (SSM_GROUPS + g + 1) * SSM_STATE]
        bm16 = bm.astype(bf16)
        cm16 = cm.astype(bf16)
        cb = _dot_nt(cm16, bm16)
        s_prev = st_scr[g]
        y_off = _dot(cm16, s_prev.astype(bf16))
        xs_scaled = []
        cd_rows = []
        for p in range(4):
            pair = g * 4 + p
            x_pair = xs[:, pair * LANES:(pair + 1) * LANES]
            x16 = x_pair.astype(bf16)
            yy = []
            for half in range(2):
                hd = 2 * pair + half
                seg = a_cum[:, hd:hd + 1] - a_cum_t[hd:hd + 1, :]
                decay = jnp.exp(jnp.where(causal, seg, NEG_INF))
                wmat = cb * decay * dt_t[hd:hd + 1, :]
                yy.append(_dot(wmat.astype(bf16), x16))
            hd0, hd1 = 2 * pair, 2 * pair + 1
            y_diag = _pair_select(lane_lo, yy[0], yy[1])
            e_pair = _pair_select(lane_lo, e_cum[:, hd0:hd0 + 1], e_cum[:, hd1:hd1 + 1])
            y_parts.append(y_diag + y_off[:, p * LANES:(p + 1) * LANES] * e_pair)
            te_pair = _pair_select(lane_lo, to_end[:, hd0:hd0 + 1], to_end[:, hd1:hd1 + 1])
            xs_scaled.append((x_pair * te_pair).astype(bf16))
            cd_rows.append(_pair_select(lane_lo_row, cdec[:, hd0:hd0 + 1], cdec[:, hd1:hd1 + 1]))
        xs_sc = jnp.concatenate(xs_scaled, axis=1)
        s_new = _dot(bm.T.astype(bf16), xs_sc)
        st_scr[g] = jnp.concatenate(cd_rows, axis=1) * s_prev + s_new
    y = jnp.concatenate(y_parts, axis=1) + dskip_ref[...] * xs
    y_ssd = _rms_norm(y * _silu(z_ref[...]), gssm_ref[...])
    o_ref[:, pl.ds(0, D_SSM)] = y_ssd.astype(o_ref.dtype)

    kcat = jnp.concatenate([kp_scr[...], k_cur], axis=0)
    vcat = jnp.concatenate([vp_scr[...], v_cur], axis=0)
    kp_scr[...] = k_cur
    vp_scr[...] = v_cur
    kroll = pltpu.roll(kcat, HEADDIM, axis=1)
    vroll = pltpu.roll(vcat, HEADDIM, axis=1)
    lane_lo2 = lax.broadcasted_iota(i32, (2 * L, LANES), 1) < HEADDIM
    qi = lax.broadcasted_iota(i32, (L, 2 * L), 0)
    kj = lax.broadcasted_iota(i32, (L, 2 * L), 1)
    dist = qi + L - kj
    valid = (dist >= 0) & (dist < L) & ((c > 0) | (kj >= L))
    att_parts = []
    for kvh in range(ATT_KV_HEADS):
        if kvh == 0:
            kd = jnp.where(lane_lo2, kcat, kroll).astype(bf16)
            vd = jnp.where(lane_lo2, vcat, vroll).astype(bf16)
        else:
            kd = jnp.where(lane_lo2, kroll, kcat).astype(bf16)
            vd = jnp.where(lane_lo2, vroll, vcat).astype(bf16)
        for p in range(4):
            pair = kvh * 4 + p
            q_pair = q_ref[:, pl.ds(pair * LANES, LANES)] * (HEADDIM ** -0.5)
            oo = []
            for half in range(2):
                hd = 2 * pair + half
                keep = lane_lo if half == 0 else jnp.logical_not(lane_lo)
                qm = jnp.where(keep, q_pair, 0.0).astype(bf16)
                s = _dot_nt(qm, kd)
                s = jnp.where(valid, s, NEG_INF)
                sink = sink_ref[hd]
                m = jnp.maximum(jnp.max(s, axis=1, keepdims=True), sink)
                pr = jnp.exp(s - m)
                denom = jnp.sum(pr, axis=1, keepdims=True) + jnp.exp(sink - m)
                oo.append(_dot((pr / denom).astype(bf16), vd))
            att_parts.append(_pair_select(lane_lo, oo[0], oo[1]))
    att = jnp.concatenate(att_parts, axis=1)
    o_ref[:, pl.ds(D_SSM, D_ATT)] = _rms_norm(att, gatt_ref[...]).astype(o_ref.dtype)


def _mixer(proj, sinks, conv_w8, conv_b, dtb, alog, dskip, gssm, gatt, batch, seq):
    t = proj.shape[0]
    nc = seq // CHUNK

    def tok(col):
        return lambda b, c: (b * nc + c, col)

    const = lambda b, c: (0, 0)
    return pl.pallas_call(
        _mixer_kernel,
        grid=(batch, nc),
        in_specs=[pl.BlockSpec(memory_space=pltpu.SMEM),
                  pl.BlockSpec((CHUNK, D_SSM), tok(0)),
                  pl.BlockSpec((CHUNK, D_SSM), tok(1)),
                  pl.BlockSpec((CHUNK, D_ATT), tok(2)),
                  pl.BlockSpec((CHUNK, D_BC), tok(6)),
                  pl.BlockSpec((CHUNK, D_BC), tok(7)),
                  pl.BlockSpec((SUBLANES, D_CONV), const),
                  pl.BlockSpec((1, D_CONV), const),
                  pl.BlockSpec((1, LANES), const),
                  pl.BlockSpec((1, LANES), const),
                  pl.BlockSpec((1, D_SSM), const),
                  pl.BlockSpec((1, D_SSM), const),
                  pl.BlockSpec((1, D_ATT), const)],
        out_specs=pl.BlockSpec((CHUNK, D_MODEL), lambda b, c: (b * nc + c, 0)),
        out_shape=jax.ShapeDtypeStruct((t, D_MODEL), bf16),
        scratch_shapes=[pltpu.VMEM((CONV_TAIL + CHUNK, D_CONV), f32),
                        pltpu.VMEM((SSM_GROUPS, SSM_STATE, D_SSM // SSM_GROUPS), f32),
                        pltpu.VMEM((CHUNK, D_KV), f32),
                        pltpu.VMEM((CHUNK, D_KV), f32)],
        compiler_params=pltpu.CompilerParams(
            dimension_semantics=("arbitrary", "arbitrary"),
            vmem_limit_bytes=40 * 1024 * 1024),
        name="mixer",
    )(sinks, proj, proj, proj, proj, proj, conv_w8, conv_b, dtb, alog, dskip, gssm, gatt)


OUT_TM = 256


PEER_UNITS = 2 * PEER_HEADS


def _outproj_kernel(y_ref, w_ref, x_ref, gate_ref, g_ref, b_ref, sh_ref, sc_ref, wq_ref,
                    x1_ref, h2_ref, q_ref):
    y = _dot(y_ref[...], w_ref[...])
    x1 = _layer_norm(ALPHA * x_ref[...] + gate_ref[0] * y, g_ref[...], b_ref[...])
    x1_ref[...] = x1
    h2 = x1 * (1.0 + sc_ref[0]) + sh_ref[0]
    h2_ref[...] = h2
    q = _dot(h2.astype(bf16), wq_ref[...])
    for u in range(PEER_UNITS):
        q_ref[u] = q[:, u * PEER_HALF:(u + 1) * PEER_HALF].astype(bf16)


def _outproj(ycat, w_out, x2, mod3, ln_g, ln_b, w_q, seq):
    t = x2.shape[0]
    per_b = seq // OUT_TM
    tok = lambda i: (i, 0)
    const = lambda i: (0, 0)

    def modspec(k):
        return pl.BlockSpec((1, 1, D_MODEL), lambda i: (i // per_b, 0, k))

    def weight():
        return pl.BlockSpec((D_MODEL, D_MODEL), const, pipeline_mode=pl.Buffered(1))

    return pl.pallas_call(
        _outproj_kernel,
        grid=(t // OUT_TM,),
        in_specs=[pl.BlockSpec((OUT_TM, D_MODEL), tok),
                  weight(),
                  pl.BlockSpec((OUT_TM, D_MODEL), tok),
                  modspec(2), pl.BlockSpec((1, D_MODEL), const), pl.BlockSpec((1, D_MODEL), const),
                  modspec(3), modspec(4),
                  weight()],
        out_specs=[pl.BlockSpec((OUT_TM, D_MODEL), tok), pl.BlockSpec((OUT_TM, D_MODEL), tok),
                   pl.BlockSpec((PEER_UNITS, OUT_TM, PEER_HALF), lambda i: (0, i, 0))],
        out_shape=[jax.ShapeDtypeStruct((t, D_MODEL), f32), jax.ShapeDtypeStruct((t, D_MODEL), f32),
                   jax.ShapeDtypeStruct((PEER_UNITS, t, PEER_HALF), bf16)],
        compiler_params=pltpu.CompilerParams(vmem_limit_bytes=48 * 1024 * 1024),
        name="outproj",
    )(ycat, w_out, x2, mod3, ln_g, ln_b, mod3, mod3, w_q)


ORDER_LAST = 2 ** 30


def _topk_rows(s, k, order, payload=None):
    vals, outs = [], []
    for _ in range(k):
        m = jnp.max(s, axis=0, keepdims=True)
        am = jnp.min(jnp.where(s == m, order, ORDER_LAST), axis=0, keepdims=True)
        hit = order == am
        vals.append(m)
        if payload is None:
            outs.append(am)
        else:
            outs.append(jnp.max(jnp.where(hit, payload, -1), axis=0, keepdims=True))
        s = jnp.where(hit, NEG_INF, s)
    return jnp.concatenate(vals, axis=0), jnp.concatenate(outs, axis=0)


def _candidate_slabs(v1, i1, v2, i2):
    tm = v1.shape[1]
    cands, orders, experts = [], [], []

    def slab(rows, vals, flat, eid, live):
        cands.append(jnp.where(live, vals, NEG_INF))
        orders.append(jnp.where(live, flat, ORDER_LAST))
        experts.append(eid)

    for k1, rows, n_live in ((0, 16, 16), (1, 8, 8), (2, 8, 5), (3, 8, 4)):
        k2 = lax.broadcasted_iota(i32, (rows, tm), 0)
        slab(rows, v1[k1:k1 + 1] + v2[0:rows], k1 * PEER_TOPK + k2,
             i1[k1:k1 + 1] * PEER_KEYS + i2[0:rows], k2 < n_live)
    for k2, rows, hi in ((0, 16, 16), (1, 8, 8), (2, 8, 5)):
        k1 = lax.broadcasted_iota(i32, (rows, tm), 0)
        slab(rows, v1[0:rows] + v2[k2:k2 + 1], k1 * PEER_TOPK + k2,
             i1[0:rows] * PEER_KEYS + i2[k2:k2 + 1], (k1 >= 4) & (k1 < hi))
    return jnp.concatenate(cands, axis=0), jnp.concatenate(orders, axis=0), jnp.concatenate(experts, axis=0)


def _route_head(q_ref, keys_ref, h):
    tm = q_ref.shape[1]
    key_id = lax.broadcasted_iota(i32, (PEER_KEYS, tm), 0)
    vals, idxs = [], []
    for cpart in range(2):
        unit = h * 2 + cpart
        sc = _dot_nt(keys_ref[unit], q_ref[unit])
        v, ix = _topk_rows(sc, PEER_TOPK, key_id)
        vals.append(v)
        idxs.append(ix)
    cand, order, cidx = _candidate_slabs(vals[0], idxs[0], vals[1], idxs[1])
    best, expert = _topk_rows(cand, PEER_TOPK, order, payload=cidx)
    ex = jnp.exp(best - jnp.max(best, axis=0, keepdims=True))
    return expert, ex / jnp.sum(ex, axis=0, keepdims=True)


EXP_TB = 128
EXP_SUB = 8
EXP_NSUB = EXP_TB // EXP_SUB
EXP_ROWS = EXP_SUB * PEER_SEL
D_UV = 2 * D_MODEL


ROUTE_SUBS = PEER_HEADS
IDX_PUBLISH = ROUTE_SUBS
IDX_READY = EXP_NSUB - 2


def _peer_kernel(q_ref, keys_ref, h_ref, x1_ref, gate_ref, g_ref, b_ref, uv_hbm, o_ref,
                 buf, y_scr, e_scr, et_scr, gate_scr, idx_smem, sem, idx_sem):
    step = pl.program_id(0)
    routes = step < pl.num_programs(0) - 1
    gathers = step >= 1
    route_buf = step % 2
    gather_buf = 1 - route_buf

    def issue(ids_buf, j, slot):
        def body(tk, carry):
            for k in range(PEER_SEL):
                e = idx_smem[ids_buf, j * EXP_SUB + tk, k]
                pltpu.make_async_copy(uv_hbm.at[e], buf.at[slot, pl.ds(tk * PEER_SEL + k, 1)],
                                      sem.at[slot]).start()
            return carry
        lax.fori_loop(0, EXP_SUB, body, 0)

    def wait(slot):
        pltpu.make_async_copy(buf.at[slot], buf.at[slot], sem.at[slot]).wait()

    def idx_copy():
        return pltpu.make_async_copy(et_scr, idx_smem.at[route_buf], idx_sem)

    def sub_block(j, carry):
        slot = j % 2

        @pl.when(gathers & (j + 1 < EXP_NSUB))
        def _():
            issue(gather_buf, j + 1, 1 - slot)

        @pl.when(routes & (j + 1 == EXP_NSUB))
        def _():
            issue(route_buf, 0, 1 - slot)

        @pl.when(routes & (j < ROUTE_SUBS))
        def _():
            expert, gate = _route_head(q_ref, keys_ref, j)
            rows = pl.ds(pl.multiple_of(j * PEER_TOPK, PEER_TOPK), PEER_TOPK)
            e_scr[rows, :] = expert
            gate_scr[route_buf, rows, :] = gate

        @pl.when(routes & (j == IDX_PUBLISH))
        def _():
            et_scr[...] = e_scr[...].T
            idx_copy().start()

        @pl.when(routes & (j == IDX_READY))
        def _():
            idx_copy().wait()

        @pl.when(gathers)
        def _():
            wait(slot)
            gs = pltpu.roll(gate_scr[gather_buf], (EXP_TB - j * EXP_SUB) & (EXP_TB - 1), axis=1)
            for tk in range(EXP_SUB):
                rows = pl.ds(tk * PEER_SEL, PEER_SEL)
                tok_row = pl.ds(j * EXP_SUB + tk, 1)
                a = jnp.sum(buf[slot, rows, pl.ds(0, D_MODEL)] * h_ref[tok_row, :], axis=1, keepdims=True)
                wgt = gs[:, tk:tk + 1] * _gelu(a)
                y_scr[tok_row, :] = jnp.sum(buf[slot, rows, pl.ds(D_MODEL, D_MODEL)] * wgt, axis=0,
                                            keepdims=True)

        return carry

    lax.fori_loop(0, EXP_NSUB, sub_block, 0)

    @pl.when(gathers)
    def _():
        o_ref[...] = _layer_norm(ALPHA * x1_ref[...] + gate_ref[0] * y_scr[...], g_ref[...], b_ref[...])


def _peer(q_units, keys, h2, x1, mod3, ln_g, ln_b, uv_tab, seq):
    t = h2.shape[0]
    n_blocks = t // EXP_TB
    per_b = seq // EXP_TB
    assert EXP_NSUB % 2 == 0, "the cross-step prefetch assumes every grid step starts on slot 0"
    assert ROUTE_SUBS <= IDX_PUBLISH < IDX_READY < EXP_NSUB - 1
    gathered = lambda s: jnp.maximum(s - 1, 0)
    tok = lambda s: (gathered(s), 0)
    const = lambda s: (0, 0)
    return pl.pallas_call(
        _peer_kernel,
        grid=(n_blocks + 1,),
        in_specs=[pl.BlockSpec((PEER_UNITS, EXP_TB, PEER_HALF), lambda s: (0, jnp.minimum(s, n_blocks - 1), 0)),
                  pl.BlockSpec((PEER_UNITS, PEER_KEYS, PEER_HALF), lambda s: (0, 0, 0)),
                  pl.BlockSpec((EXP_TB, D_MODEL), tok),
                  pl.BlockSpec((EXP_TB, D_MODEL), tok),
                  pl.BlockSpec((1, 1, D_MODEL), lambda s: (gathered(s) // per_b, 0, 5)),
                  pl.BlockSpec((1, D_MODEL), const),
                  pl.BlockSpec((1, D_MODEL), const),
                  pl.BlockSpec(memory_space=pl.ANY)],
        out_specs=pl.BlockSpec((EXP_TB, D_MODEL), tok),
        out_shape=jax.ShapeDtypeStruct((t, D_MODEL), f32),
        scratch_shapes=[pltpu.VMEM((2, EXP_ROWS, D_UV), f32),
                        pltpu.VMEM((EXP_TB, D_MODEL), f32),
                        pltpu.VMEM((PEER_SEL, EXP_TB), i32),
                        pltpu.VMEM((EXP_TB, PEER_SEL), i32),
                        pltpu.VMEM((2, PEER_SEL, EXP_TB), f32),
                        pltpu.SMEM((2, EXP_TB, PEER_SEL), i32),
                        pltpu.SemaphoreType.DMA((2,)),
                        pltpu.SemaphoreType.DMA(())],
        compiler_params=pltpu.CompilerParams(
            dimension_semantics=("arbitrary",),
            vmem_limit_bytes=56 * 1024 * 1024),
        name="peer",
    )(q_units, keys, h2, x1, mod3, ln_g, ln_b, uv_tab)


def _regroup_w_in(w):
    o_xbc = D_SSM
    o_dt = o_xbc + D_CONV
    o_q = o_dt + SSM_HEADS
    o_k = o_q + D_ATT
    o_v = o_k + D_KV
    pad = jnp.zeros((w.shape[0], D_PROJ - (2 * D_SSM + D_ATT + D_BC + 2 * D_KV + SSM_HEADS)), w.dtype)
    return jnp.concatenate([w[:, :D_SSM], w[:, o_xbc:o_xbc + D_SSM], w[:, o_q:o_k],
                            w[:, o_xbc + D_SSM:o_dt], w[:, o_k:o_v], w[:, o_v:o_v + D_KV],
                            w[:, o_dt:o_q], pad], axis=1)


def _pad_lanes(v):
    return jnp.pad(v, (0, LANES - v.shape[0]))[None, :]


def kernel(x, c, w_ada, b_ada, w_in, conv_w, conv_b, dt_bias, a_log, d_skip, ssm_norm_g, attn_sinks, attn_norm_g, w_out, ln1_g, ln1_b, peer_w_q, peer_sub_keys, peer_u, peer_v, ln2_g, ln2_b):
    batch, seq, d = x.shape
    t = batch * seq
    x2 = x.reshape(t, d)
    for l in range(DEPTH):
        c_pad = jnp.pad(c, ((0, SUBLANES - batch), (0, 0)))
        mod = _ada(c_pad, w_ada[l], b_ada[l][None, :])
        mod3 = mod[:batch].reshape(batch, 1, N_MOD * d)
        proj = _inproj(x2, mod3, _regroup_w_in(w_in[l]).astype(bf16), seq)
        conv_w8 = jnp.pad(conv_w[l], ((0, SUBLANES - CONV_WIDTH), (0, 0)))
        ycat = _mixer(proj, attn_sinks[l], conv_w8, conv_b[l][None, :], _pad_lanes(dt_bias[l]),
                      _pad_lanes(a_log[l]), jnp.repeat(d_skip[l], HEADDIM)[None, :],
                      ssm_norm_g[l][None, :], attn_norm_g[l][None, :], batch, seq)
        x1, h2, q_units = _outproj(ycat, w_out[l].astype(bf16), x2, mod3, ln1_g[l][None, :], ln1_b[l][None, :],
                                   peer_w_q[l].astype(bf16), seq)
        keys = peer_sub_keys[l].astype(bf16).reshape(PEER_UNITS, PEER_KEYS, PEER_HALF)
        uv_tab = jnp.concatenate([peer_u[l][:, None, :], peer_v[l][:, None, :]], axis=2)
        x2 = _peer(q_units, keys, h2, x1, mod3, ln2_g[l][None, :], ln2_b[l][None, :], uv_tab, seq)
    return x2.reshape(batch, seq, d)
```

```python
import functools

import jax
import jax.numpy as jnp
from jax import lax
from jax.experimental import pallas as pl
from jax.experimental.pallas import tpu as pltpu

f32 = jnp.float32
bf16 = jnp.bfloat16
i32 = jnp.int32
u32 = jnp.uint32

D_MODEL = 2048
D_SSM = 1024
D_ATT = 1024
SSM_HEADS = 16
SSM_GROUPS = 2
SSM_STATE = 128
HEADDIM = 64
CONV_WIDTH = 4
CHUNK = 128
D_BC = 2 * SSM_GROUPS * SSM_STATE
D_CONV = D_SSM + D_BC
ATT_HEADS = 16
ATT_KV_HEADS = 2
D_KV = ATT_KV_HEADS * HEADDIM
PEER_HEADS = 8
PEER_KEYS = 128
PEER_TOPK = 16
PEER_HALF = 128
PEER_SEL = PEER_HEADS * PEER_TOPK
N_MOD = 6
DEPTH = 1
ALPHA = (2.0 * DEPTH) ** 0.25
EPS = 1e-5

LANES = 128
SUBLANES = 8
D_PROJ = 4096
HALF_D = D_MODEL // 2
NEG_INF = float("-inf")


def _silu(x):
    return x * jax.nn.sigmoid(x)


def _gelu(x):
    return 0.5 * x * (1.0 + lax.erf(x * (2.0 ** -0.5)))


def _dot(a, b):
    return jnp.dot(a, b, preferred_element_type=f32)


def _dot_nt(a, b):
    return lax.dot_general(a, b, (((1,), (1,)), ((), ())), preferred_element_type=f32)


def _layer_norm(x, g, b):
    mu = jnp.mean(x, axis=-1, keepdims=True)
    xc = x - mu
    var = jnp.mean(xc * xc, axis=-1, keepdims=True)
    return xc * lax.rsqrt(var + EPS) * g + b


def _rms_norm(x, g):
    return x * lax.rsqrt(jnp.mean(x * x, axis=-1, keepdims=True) + EPS) * g


ADA_TN = 1024


def _ada_kernel(c_ref, w_ref, b_ref, o_ref):
    sc = _silu(c_ref[...])
    o_ref[...] = _dot(sc.astype(bf16), w_ref[...].astype(bf16)) + b_ref[...]


def _ada(c_pad, w_ada, b_ada):
    n = w_ada.shape[1]
    return pl.pallas_call(
        _ada_kernel,
        grid=(n // ADA_TN,),
        in_specs=[pl.BlockSpec((SUBLANES, D_MODEL), lambda j: (0, 0)),
                  pl.BlockSpec((D_MODEL, ADA_TN), lambda j: (0, j)),
                  pl.BlockSpec((1, ADA_TN), lambda j: (0, j))],
        out_specs=pl.BlockSpec((SUBLANES, ADA_TN), lambda j: (0, j)),
        out_shape=jax.ShapeDtypeStruct((SUBLANES, n), f32),
        compiler_params=pltpu.CompilerParams(vmem_limit_bytes=40 * 1024 * 1024),
        name="ada",
    )(c_pad, w_ada, b_ada)


INPROJ_TM = 512
INPROJ_TN = 1024


def _inproj_kernel(x_ref, sh_ref, sc_ref, w_ref, o_ref, h_scr):
    @pl.when(pl.program_id(1) == 0)
    def _():
        h = x_ref[...] * (1.0 + sc_ref[0]) + sh_ref[0]
        h_scr[...] = h.astype(bf16)

    o_ref[...] = _dot(h_scr[...], w_ref[...])


def _inproj(x2, mod3, w_cat, seq):
    t = x2.shape[0]
    per_b = seq // INPROJ_TM
    return pl.pallas_call(
        _inproj_kernel,
        grid=(t // INPROJ_TM, D_PROJ // INPROJ_TN),
        in_specs=[pl.BlockSpec((INPROJ_TM, D_MODEL), lambda i, j: (i, 0)),
                  pl.BlockSpec((1, 1, D_MODEL), lambda i, j: (i // per_b, 0, 0)),
                  pl.BlockSpec((1, 1, D_MODEL), lambda i, j: (i // per_b, 0, 1)),
                  pl.BlockSpec((D_MODEL, INPROJ_TN), lambda i, j: (0, j))],
        out_specs=pl.BlockSpec((INPROJ_TM, INPROJ_TN), lambda i, j: (i, j)),
        out_shape=jax.ShapeDtypeStruct((t, D_PROJ), f32),
        scratch_shapes=[pltpu.VMEM((INPROJ_TM, D_MODEL), bf16)],
        compiler_params=pltpu.CompilerParams(
            dimension_semantics=("arbitrary", "arbitrary"),
            vmem_limit_bytes=40 * 1024 * 1024),
        name="inproj",
    )(x2, mod3, mod3, w_cat)


CONV_TAIL = SUBLANES


def _split3(a):
    hi = a.astype(bf16)
    r1 = a - hi.astype(f32)
    mid = r1.astype(bf16)
    lo = (r1 - mid.astype(f32)).astype(bf16)
    return hi, mid, lo


def _pair_select(lane_lo, even, odd):
    return jnp.where(lane_lo, even, odd)


def _mixer_kernel(sink_ref, z_ref, xs_ref, q_ref, bc_ref, kvdt_ref, cw_ref, cb_ref, dtb_ref, alog_ref,
                  dskip_ref, gssm_ref, gatt_ref, o_ref, ext_scr, st_scr, kp_scr, vp_scr):
    c = pl.program_id(1)
    L = CHUNK

    @pl.when(c == 0)
    def _():
        ext_scr[pl.ds(0, CONV_TAIL), :] = jnp.zeros((CONV_TAIL, D_CONV), f32)
        st_scr[...] = jnp.zeros(st_scr.shape, f32)
        kp_scr[...] = jnp.zeros(kp_scr.shape, f32)
        vp_scr[...] = jnp.zeros(vp_scr.shape, f32)

    ext_scr[pl.ds(CONV_TAIL, L), pl.ds(0, D_SSM)] = xs_ref[...]
    ext_scr[pl.ds(CONV_TAIL, L), pl.ds(D_SSM, D_BC)] = bc_ref[...]
    conv = jnp.zeros((L, D_CONV), f32) + cb_ref[...]
    for w in range(CONV_WIDTH):
        off = CONV_TAIL - (CONV_WIDTH - 1) + w
        conv = conv + ext_scr[pl.ds(off, L), :] * cw_ref[pl.ds(w, 1), :]
    tail = ext_scr[pl.ds(L, CONV_TAIL), :]
    ext_scr[pl.ds(0, CONV_TAIL), :] = tail
    xbc = _silu(conv)
    xs = xbc[:, :D_SSM]

    kvdt = kvdt_ref[...]
    k_cur = kvdt[:, 0:D_KV]
    v_cur = kvdt[:, D_KV:2 * D_KV]
    dt_raw = kvdt[:, 2 * D_KV:3 * D_KV]

    xdt = dt_raw + dtb_ref[...]
    dt = jnp.maximum(xdt, 0.0) + jnp.log1p(jnp.exp(-jnp.abs(xdt)))
    a = dt * (-jnp.exp(alog_ref[...]))
    row_i = lax.broadcasted_iota(i32, (L, L), 0)
    col_i = lax.broadcasted_iota(i32, (L, L), 1)
    causal = row_i >= col_i
    tri = jnp.where(causal, 1.0, 0.0).astype(bf16)
    a_hi, a_mid, a_lo = _split3(a)
    a_cum = _dot(tri, a_hi) + _dot(tri, a_mid) + _dot(tri, a_lo)
    a_cum_t = a_cum.T
    dt_t = dt.T
    a_last = a_cum[L - 1:L, :]
    e_cum = jnp.exp(a_cum)
    to_end = jnp.exp(a_last - a_cum) * dt
    cdec = jnp.exp(a_last)
    lane_lo = lax.broadcasted_iota(i32, (L, LANES), 1) < HEADDIM
    lane_lo_row = lax.broadcasted_iota(i32, (1, LANES), 1) < HEADDIM

    y_parts = []
    for g in range(SSM_GROUPS):
        bm = xbc[:, D_SSM + g * SSM_STATE:D_SSM + (g + 1) * SSM_STATE]
        cm = xbc[:, D_SSM + (SSM_GROUPS + g) * SSM_STATE:D_SSM + (SSM_GROUPS + g + 1) * SSM_STATE]
        bm16 = bm.astype(bf16)
        cm16 = cm.astype(bf16)
        cb = _dot_nt(cm16, bm16)
        s_prev = st_scr[g]
        y_off = _dot(cm16, s_prev.astype(bf16))
        xs_scaled = []
        cd_rows = []
        for p in range(4):
            pair = g * 4 + p
            x_pair = xs[:, pair * LANES:(pair + 1) * LANES]
            x16 = x_pair.astype(bf16)
            yy = []
            for half in range(2):
                hd = 2 * pair + half
                seg = a_cum[:, hd:hd + 1] - a_cum_t[hd:hd + 1, :]
                decay = jnp.exp(jnp.where(causal, seg, NEG_INF))
                wmat = cb * decay * dt_t[hd:hd + 1, :]
                yy.append(_dot(wmat.astype(bf16), x16))
            hd0, hd1 = 2 * pair, 2 * pair + 1
            y_diag = _pair_select(lane_lo, yy[0], yy[1])
            e_pair = _pair_select(lane_lo, e_cum[:, hd0:hd0 + 1], e_cum[:, hd1:hd1 + 1])
            y_parts.append(y_diag + y_off[:, p * LANES:(p + 1) * LANES] * e_pair)
            te_pair = _pair_select(lane_lo, to_end[:, hd0:hd0 + 1], to_end[:, hd1:hd1 + 1])
            xs_scaled.append((x_pair * te_pair).astype(bf16))
            cd_rows.append(_pair_select(lane_lo_row, cdec[:, hd0:hd0 + 1], cdec[:, hd1:hd1 + 1]))
        xs_sc = jnp.concatenate(xs_scaled, axis=1)
        s_new = _dot(bm.T.astype(bf16), xs_sc)
        st_scr[g] = jnp.concatenate(cd_rows, axis=1) * s_prev + s_new
    y = jnp.concatenate(y_parts, axis=1) + dskip_ref[...] * xs
    y_ssd = _rms_norm(y * _silu(z_ref[...]), gssm_ref[...])
    o_ref[:, pl.ds(0, D_SSM)] = y_ssd.astype(o_ref.dtype)

    kcat = jnp.concatenate([kp_scr[...], k_cur], axis=0)
    vcat = jnp.concatenate([vp_scr[...], v_cur], axis=0)
    kp_scr[...] = k_cur
    vp_scr[...] = v_cur
    kroll = pltpu.roll(kcat, HEADDIM, axis=1)
    vroll = pltpu.roll(vcat, HEADDIM, axis=1)
    lane_lo2 = lax.broadcasted_iota(i32, (2 * L, LANES), 1) < HEADDIM
    qi = lax.broadcasted_iota(i32, (L, 2 * L), 0)
    kj = lax.broadcasted_iota(i32, (L, 2 * L), 1)
    dist = qi + L - kj
    valid = (dist >= 0) & (dist < L) & ((c > 0) | (kj >= L))
    att_parts = []
    for kvh in range(ATT_KV_HEADS):
        if kvh == 0:
            kd = jnp.where(lane_lo2, kcat, kroll).astype(bf16)
            vd = jnp.where(lane_lo2, vcat, vroll).astype(bf16)
        else:
            kd = jnp.where(lane_lo2, kroll, kcat).astype(bf16)
            vd = jnp.where(lane_lo2, vroll, vcat).astype(bf16)
        for p in range(4):
            pair = kvh * 4 + p
            q_pair = q_ref[:, pl.ds(pair * LANES, LANES)] * (HEADDIM ** -0.5)
            oo = []
            for half in range(2):
                hd = 2 * pair + half
                keep = lane_lo if half == 0 else jnp.logical_not(lane_lo)
                qm = jnp.where(keep, q_pair, 0.0).astype(bf16)
                s = _dot_nt(qm, kd)
                s = jnp.where(valid, s, NEG_INF)
                sink = sink_ref[hd]
                m = jnp.maximum(jnp.max(s, axis=1, keepdims=True), sink)
                pr = jnp.exp(s - m)
                denom = jnp.sum(pr, axis=1, keepdims=True) + jnp.exp(sink - m)
                oo.append(_dot((pr / denom).astype(bf16), vd))
            att_parts.append(_pair_select(lane_lo, oo[0], oo[1]))
    att = jnp.concatenate(att_parts, axis=1)
    o_ref[:, pl.ds(D_SSM, D_ATT)] = _rms_norm(att, gatt_ref[...]).astype(o_ref.dtype)


def _mixer(proj, sinks, conv_w8, conv_b, dtb, alog, dskip, gssm, gatt, batch, seq):
    t = proj.shape[0]
    nc = seq // CHUNK

    def tok(col):
        return lambda b, c: (b * nc + c, col)

    const = lambda b, c: (0, 0)
    return pl.pallas_call(
        _mixer_kernel,
        grid=(batch, nc),
        in_specs=[pl.BlockSpec(memory_space=pltpu.SMEM),
                  pl.BlockSpec((CHUNK, D_SSM), tok(0)),
                  pl.BlockSpec((CHUNK, D_SSM), tok(1)),
                  pl.BlockSpec((CHUNK, D_ATT), tok(2)),
                  pl.BlockSpec((CHUNK, D_BC), tok(6)),
                  pl.BlockSpec((CHUNK, D_BC), tok(7)),
                  pl.BlockSpec((SUBLANES, D_CONV), const),
                  pl.BlockSpec((1, D_CONV), const),
                  pl.BlockSpec((1, LANES), const),
                  pl.BlockSpec((1, LANES), const),
                  pl.BlockSpec((1, D_SSM), const),
                  pl.BlockSpec((1, D_SSM), const),
                  pl.BlockSpec((1, D_ATT), const)],
        out_specs=pl.BlockSpec((CHUNK, D_MODEL), lambda b, c: (b * nc + c, 0)),
        out_shape=jax.ShapeDtypeStruct((t, D_MODEL), bf16),
        scratch_shapes=[pltpu.VMEM((CONV_TAIL + CHUNK, D_CONV), f32),
                        pltpu.VMEM((SSM_GROUPS, SSM_STATE, D_SSM // SSM_GROUPS), f32),
                        pltpu.VMEM((CHUNK, D_KV), f32),
                        pltpu.VMEM((CHUNK, D_KV), f32)],
        compiler_params=pltpu.CompilerParams(
            dimension_semantics=("arbitrary", "arbitrary"),
            vmem_limit_bytes=40 * 1024 * 1024),
        name="mixer",
    )(sinks, proj, proj, proj, proj, proj, conv_w8, conv_b, dtb, alog, dskip, gssm, gatt)


OUT_TM = 256


PEER_UNITS = 2 * PEER_HEADS


def _outproj_kernel(y_ref, w_ref, x_ref, gate_ref, g_ref, b_ref, sh_ref, sc_ref, wq_ref,
                    x1_ref, h2_ref, q_ref):
    y = _dot(y_ref[...], w_ref[...])
    x1 = _layer_norm(ALPHA * x_ref[...] + gate_ref[0] * y, g_ref[...], b_ref[...])
    x1_ref[...] = x1
    h2 = x1 * (1.0 + sc_ref[0]) + sh_ref[0]
    h2_ref[...] = h2
    q = _dot(h2.astype(bf16), wq_ref[...])
    for u in range(PEER_UNITS):
        q_ref[u] = q[:, u * PEER_HALF:(u + 1) * PEER_HALF].astype(bf16)


def _outproj(ycat, w_out, x2, mod3, ln_g, ln_b, w_q, seq):
    t = x2.shape[0]
    per_b = seq // OUT_TM
    tok = lambda i: (i, 0)
    const = lambda i: (0, 0)

    def modspec(k):
        return pl.BlockSpec((1, 1, D_MODEL), lambda i: (i // per_b, 0, k))

    def weight():
        return pl.BlockSpec((D_MODEL, D_MODEL), const, pipeline_mode=pl.Buffered(1))

    return pl.pallas_call(
        _outproj_kernel,
        grid=(t // OUT_TM,),
        in_specs=[pl.BlockSpec((OUT_TM, D_MODEL), tok),
                  weight(),
                  pl.BlockSpec((OUT_TM, D_MODEL), tok),
                  modspec(2), pl.BlockSpec((1, D_MODEL), const), pl.BlockSpec((1, D_MODEL), const),
                  modspec(3), modspec(4),
                  weight()],
        out_specs=[pl.BlockSpec((OUT_TM, D_MODEL), tok), pl.BlockSpec((OUT_TM, D_MODEL), tok),
                   pl.BlockSpec((PEER_UNITS, OUT_TM, PEER_HALF), lambda i: (0, i, 0))],
        out_shape=[jax.ShapeDtypeStruct((t, D_MODEL), f32), jax.ShapeDtypeStruct((t, D_MODEL), f32),
                   jax.ShapeDtypeStruct((PEER_UNITS, t, PEER_HALF), bf16)],
        compiler_params=pltpu.CompilerParams(vmem_limit_bytes=48 * 1024 * 1024),
        name="outproj",
    )(ycat, w_out, x2, mod3, ln_g, ln_b, mod3, mod3, w_q)


ORDER_LAST = 2 ** 30


def _topk_rows(s, k, order, payload=None, every=None, between=None):
    vals, outs = [], []
    for it in range(k):
        if between is not None and it % every == 0:
            between()
        m = jnp.max(s, axis=0, keepdims=True)
        am = jnp.min(jnp.where(s == m, order, ORDER_LAST), axis=0, keepdims=True)
        hit = order == am
        vals.append(m)
        if payload is None:
            outs.append(am)
        else:
            outs.append(jnp.max(jnp.where(hit, payload, -1), axis=0, keepdims=True))
        s = jnp.where(hit, NEG_INF, s)
    return jnp.concatenate(vals, axis=0), jnp.concatenate(outs, axis=0)


def _candidate_slabs(v1, i1, v2, i2):
    tm = v1.shape[1]
    cands, orders, experts = [], [], []

    def slab(rows, vals, flat, eid, live):
        cands.append(jnp.where(live, vals, NEG_INF))
        orders.append(jnp.where(live, flat, ORDER_LAST))
        experts.append(eid)

    for k1, rows, n_live in ((0, 16, 16), (1, 8, 8), (2, 8, 5), (3, 8, 4)):
        k2 = lax.broadcasted_iota(i32, (rows, tm), 0)
        slab(rows, v1[k1:k1 + 1] + v2[0:rows], k1 * PEER_TOPK + k2,
             i1[k1:k1 + 1] * PEER_KEYS + i2[0:rows], k2 < n_live)
    for k2, rows, hi in ((0, 16, 16), (1, 8, 8), (2, 8, 5)):
        k1 = lax.broadcasted_iota(i32, (rows, tm), 0)
        slab(rows, v1[0:rows] + v2[k2:k2 + 1], k1 * PEER_TOPK + k2,
             i1[0:rows] * PEER_KEYS + i2[k2:k2 + 1], (k1 >= 4) & (k1 < hi))
    return jnp.concatenate(cands, axis=0), jnp.concatenate(orders, axis=0), jnp.concatenate(experts, axis=0)


ROUTE_HOOK_EVERY = 4
ROUTE_HOOK_CALLS = 2 * PEER_TOPK // ROUTE_HOOK_EVERY


def _route_head(q_ref, keys_ref, h, between=None):
    tm = q_ref.shape[1]
    key_id = lax.broadcasted_iota(i32, (PEER_KEYS, tm), 0)
    vals, idxs = [], []
    for cpart in range(2):
        unit = h * 2 + cpart
        sc = _dot_nt(keys_ref[unit], q_ref[unit])
        v, ix = _topk_rows(sc, PEER_TOPK, key_id, every=ROUTE_HOOK_EVERY, between=between)
        vals.append(v)
        idxs.append(ix)
    cand, order, cidx = _candidate_slabs(vals[0], idxs[0], vals[1], idxs[1])
    best, expert = _topk_rows(cand, PEER_TOPK, order, payload=cidx)
    ex = jnp.exp(best - jnp.max(best, axis=0, keepdims=True))
    return expert, ex / jnp.sum(ex, axis=0, keepdims=True)


EXP_TB = 128
EXP_SUB = 8
EXP_NSUB = EXP_TB // EXP_SUB
EXP_ROWS = EXP_SUB * PEER_SEL
D_UV = 2 * D_MODEL


EXP_PART = 32
EXP_PARTS = PEER_SEL // EXP_PART
N_BURSTS = EXP_SUB * EXP_PARTS
ROUTE_SUBS = PEER_HEADS
IDX_PUBLISH = ROUTE_SUBS
IDX_READY = EXP_NSUB - 2


def _peer_kernel(q_ref, keys_ref, h_ref, x1_ref, gate_ref, g_ref, b_ref, uv_hbm, o_ref,
                 buf, y_scr, e_scr, et_scr, gate_scr, idx_smem, sem, idx_sem):
    step = pl.program_id(0)
    routes = step < pl.num_programs(0) - 1
    route_buf = step % 2
    gather_buf = 1 - route_buf

    def burst(b, ids_buf, sub, slot):
        tk, part = divmod(b, EXP_PARTS)
        for k in range(part * EXP_PART, (part + 1) * EXP_PART):
            e = idx_smem[ids_buf, sub * EXP_SUB + tk, k]
            pltpu.make_async_copy(uv_hbm.at[e], buf.at[slot, pl.ds(tk * PEER_SEL + k, 1)],
                                  sem.at[slot, tk]).start()

    def wait_token(slot, tk):
        rows = pl.ds(tk * PEER_SEL, PEER_SEL)
        pltpu.make_async_copy(buf.at[slot, rows], buf.at[slot, rows], sem.at[slot, tk]).wait()

    def idx_copy():
        return pltpu.make_async_copy(et_scr, idx_smem.at[route_buf], idx_sem)

    def route_head(h, between=None):
        expert, gate = _route_head(q_ref, keys_ref, h, between=between)
        rows = pl.ds(pl.multiple_of(h * PEER_TOPK, PEER_TOPK), PEER_TOPK)
        e_scr[rows, :] = expert
        gate_scr[route_buf, rows, :] = gate

    def publish_ids():
        et_scr[...] = e_scr[...].T
        idx_copy().start()

    @pl.when(step == 0)
    def _():
        def head(h, carry):
            route_head(h)
            return carry
        lax.fori_loop(0, PEER_HEADS, head, 0)
        publish_ids()
        idx_copy().wait()
        for b in range(N_BURSTS):
            burst(b, route_buf, 0, 0)

    @pl.when(step >= 1)
    def _():
        def sub_block(j, carry):
            slot = j % 2
            in_block = j + 1 < EXP_NSUB
            do_issue = in_block | routes
            ids_buf = jnp.where(in_block, gather_buf, route_buf)
            nxt = jnp.where(in_block, j + 1, 0)
            routing_sub = routes & (j < ROUTE_SUBS)

            @pl.when(routing_sub)
            def _():
                pending = iter(range(ROUTE_HOOK_CALLS))
                route_head(j, between=lambda: burst(next(pending), gather_buf, j + 1, 1 - slot))

            @pl.when(jnp.logical_not(routing_sub) & do_issue)
            def _():
                for b in range(ROUTE_HOOK_CALLS):
                    burst(b, ids_buf, nxt, 1 - slot)

            @pl.when(routes & (j == IDX_PUBLISH))
            def _():
                publish_ids()

            @pl.when(routes & (j == IDX_READY))
            def _():
                idx_copy().wait()

            gs = pltpu.roll(gate_scr[gather_buf], (EXP_TB - j * EXP_SUB) & (EXP_TB - 1), axis=1)
            per_token = (N_BURSTS - ROUTE_HOOK_CALLS) // EXP_SUB
            for tk in range(EXP_SUB):
                @pl.when(do_issue)
                def _():
                    for b in range(ROUTE_HOOK_CALLS + tk * per_token, ROUTE_HOOK_CALLS + (tk + 1) * per_token):
                        burst(b, ids_buf, nxt, 1 - slot)

                wait_token(slot, tk)
                rows = pl.ds(tk * PEER_SEL, PEER_SEL)
                tok_row = pl.ds(j * EXP_SUB + tk, 1)
                a = jnp.sum(buf[slot, rows, pl.ds(0, D_MODEL)] * h_ref[tok_row, :], axis=1, keepdims=True)
                wgt = gs[:, tk:tk + 1] * _gelu(a)
                y_scr[tok_row, :] = jnp.sum(buf[slot, rows, pl.ds(D_MODEL, D_MODEL)] * wgt, axis=0,
                                            keepdims=True)
            return carry

        lax.fori_loop(0, EXP_NSUB, sub_block, 0)
        o_ref[...] = _layer_norm(ALPHA * x1_ref[...] + gate_ref[0] * y_scr[...], g_ref[...], b_ref[...])


def _peer(q_units, keys, h2, x1, mod3, ln_g, ln_b, uv_tab, seq):
    t = h2.shape[0]
    n_blocks = t // EXP_TB
    per_b = seq // EXP_TB
    assert EXP_NSUB % 2 == 0, "the cross-step prefetch assumes every grid step starts on slot 0"
    assert ROUTE_SUBS <= IDX_PUBLISH < IDX_READY < EXP_NSUB - 1
    gathered = lambda s: jnp.maximum(s - 1, 0)
    tok = lambda s: (gathered(s), 0)
    const = lambda s: (0, 0)
    return pl.pallas_call(
        _peer_kernel,
        grid=(n_blocks + 1,),
        in_specs=[pl.BlockSpec((PEER_UNITS, EXP_TB, PEER_HALF), lambda s: (0, jnp.minimum(s, n_blocks - 1), 0)),
                  pl.BlockSpec((PEER_UNITS, PEER_KEYS, PEER_HALF), lambda s: (0, 0, 0)),
                  pl.BlockSpec((EXP_TB, D_MODEL), tok),
                  pl.BlockSpec((EXP_TB, D_MODEL), tok),
                  pl.BlockSpec((1, 1, D_MODEL), lambda s: (gathered(s) // per_b, 0, 5)),
                  pl.BlockSpec((1, D_MODEL), const),
                  pl.BlockSpec((1, D_MODEL), const),
                  pl.BlockSpec(memory_space=pl.ANY)],
        out_specs=pl.BlockSpec((EXP_TB, D_MODEL), tok),
        out_shape=jax.ShapeDtypeStruct((t, D_MODEL), f32),
        scratch_shapes=[pltpu.VMEM((2, EXP_ROWS, D_UV), f32),
                        pltpu.VMEM((EXP_TB, D_MODEL), f32),
                        pltpu.VMEM((PEER_SEL, EXP_TB), i32),
                        pltpu.VMEM((EXP_TB, PEER_SEL), i32),
                        pltpu.VMEM((2, PEER_SEL, EXP_TB), f32),
                        pltpu.SMEM((2, EXP_TB, PEER_SEL), i32),
                        pltpu.SemaphoreType.DMA((2, EXP_SUB)),
                        pltpu.SemaphoreType.DMA(())],
        compiler_params=pltpu.CompilerParams(
            dimension_semantics=("arbitrary",),
            vmem_limit_bytes=56 * 1024 * 1024),
        name="peer",
    )(q_units, keys, h2, x1, mod3, ln_g, ln_b, uv_tab)


def _regroup_w_in(w):
    o_xbc = D_SSM
    o_dt = o_xbc + D_CONV
    o_q = o_dt + SSM_HEADS
    o_k = o_q + D_ATT
    o_v = o_k + D_KV
    pad = jnp.zeros((w.shape[0], D_PROJ - (2 * D_SSM + D_ATT + D_BC + 2 * D_KV + SSM_HEADS)), w.dtype)
    return jnp.concatenate([w[:, :D_SSM], w[:, o_xbc:o_xbc + D_SSM], w[:, o_q:o_k],
                            w[:, o_xbc + D_SSM:o_dt], w[:, o_k:o_v], w[:, o_v:o_v + D_KV],
                            w[:, o_dt:o_q], pad], axis=1)


def _pad_lanes(v):
    return jnp.pad(v, (0, LANES - v.shape[0]))[None, :]


def kernel(x, c, w_ada, b_ada, w_in, conv_w, conv_b, dt_bias, a_log, d_skip, ssm_norm_g, attn_sinks, attn_norm_g, w_out, ln1_g, ln1_b, peer_w_q, peer_sub_keys, peer_u, peer_v, ln2_g, ln2_b):
    batch, seq, d = x.shape
    t = batch * seq
    x2 = x.reshape(t, d)
    for l in range(DEPTH):
        c_pad = jnp.pad(c, ((0, SUBLANES - batch), (0, 0)))
        mod = _ada(c_pad, w_ada[l], b_ada[l][None, :])
        mod3 = mod[:batch].reshape(batch, 1, N_MOD * d)
        proj = _inproj(x2, mod3, _regroup_w_in(w_in[l]).astype(bf16), seq)
        conv_w8 = jnp.pad(conv_w[l], ((0, SUBLANES - CONV_WIDTH), (0, 0)))
        ycat = _mixer(proj, attn_sinks[l], conv_w8, conv_b[l][None, :], _pad_lanes(dt_bias[l]),
                      _pad_lanes(a_log[l]), jnp.repeat(d_skip[l], HEADDIM)[None, :],
                      ssm_norm_g[l][None, :], attn_norm_g[l][None, :], batch, seq)
        x1, h2, q_units = _outproj(ycat, w_out[l].astype(bf16), x2, mod3, ln1_g[l][None, :], ln1_b[l][None, :],
                                   peer_w_q[l].astype(bf16), seq)
        keys = peer_sub_keys[l].astype(bf16).reshape(PEER_UNITS, PEER_KEYS, PEER_HALF)
        uv_tab = jnp.concatenate([peer_u[l][:, None, :], peer_v[l][:, None, :]], axis=2)
        x2 = _peer(q_units, keys, h2, x1, mod3, ln2_g[l][None, :], ln2_b[l][None, :], uv_tab, seq)
    return x2.reshape(batch, seq, d)
```

```python
import functools

import jax
import jax.numpy as jnp
from jax import lax
from jax.experimental import pallas as pl
from jax.experimental.pallas import tpu as pltpu

f32 = jnp.float32
bf16 = jnp.bfloat16
i32 = jnp.int32
u32 = jnp.uint32

D_MODEL = 2048
D_SSM = 1024
D_ATT = 1024
SSM_HEADS = 16
SSM_GROUPS = 2
SSM_STATE = 128
HEADDIM = 64
CONV_WIDTH = 4
CHUNK = 128
D_BC = 2 * SSM_GROUPS * SSM_STATE
D_CONV = D_SSM + D_BC
ATT_HEADS = 16
ATT_KV_HEADS = 2
D_KV = ATT_KV_HEADS * HEADDIM
PEER_HEADS = 8
PEER_KEYS = 128
PEER_TOPK = 16
PEER_HALF = 128
PEER_SEL = PEER_HEADS * PEER_TOPK
N_MOD = 6
DEPTH = 1
ALPHA = (2.0 * DEPTH) ** 0.25
EPS = 1e-5

LANES = 128
SUBLANES = 8
D_PROJ = 4096
HALF_D = D_MODEL // 2
NEG_INF = float("-inf")


def _silu(x):
    return x * jax.nn.sigmoid(x)


def _gelu(x):
    return 0.5 * x * (1.0 + lax.erf(x * (2.0 ** -0.5)))


def _dot(a, b):
    return jnp.dot(a, b, preferred_element_type=f32)


def _dot_nt(a, b):
    return lax.dot_general(a, b, (((1,), (1,)), ((), ())), preferred_element_type=f32)


def _layer_norm(x, g, b):
    mu = jnp.mean(x, axis=-1, keepdims=True)
    xc = x - mu
    var = jnp.mean(xc * xc, axis=-1, keepdims=True)
    return xc * lax.rsqrt(var + EPS) * g + b


def _rms_norm(x, g):
    return x * lax.rsqrt(jnp.mean(x * x, axis=-1, keepdims=True) + EPS) * g


ADA_TN = 1024


def _ada_kernel(c_ref, w_ref, b_ref, o_ref):
    sc = _silu(c_ref[...])
    o_ref[...] = _dot(sc.astype(bf16), w_ref[...].astype(bf16)) + b_ref[...]


def _ada(c_pad, w_ada, b_ada):
    n = w_ada.shape[1]
    return pl.pallas_call(
        _ada_kernel,
        grid=(n // ADA_TN,),
        in_specs=[pl.BlockSpec((SUBLANES, D_MODEL), lambda j: (0, 0)),
                  pl.BlockSpec((D_MODEL, ADA_TN), lambda j: (0, j)),
                  pl.BlockSpec((1, ADA_TN), lambda j: (0, j))],
        out_specs=pl.BlockSpec((SUBLANES, ADA_TN), lambda j: (0, j)),
        out_shape=jax.ShapeDtypeStruct((SUBLANES, n), f32),
        compiler_params=pltpu.CompilerParams(vmem_limit_bytes=40 * 1024 * 1024),
        name="ada",
    )(c_pad, w_ada, b_ada)


INPROJ_TM = 512
INPROJ_TN = 1024


def _inproj_kernel(x_ref, sh_ref, sc_ref, w_ref, o_ref):
    h = (x_ref[...] * (1.0 + sc_ref[0]) + sh_ref[0]).astype(bf16)
    for n in range(D_PROJ // INPROJ_TN):
        cols = pl.ds(n * INPROJ_TN, INPROJ_TN)
        o_ref[:, cols] = _dot(h, w_ref[:, cols])


def _inproj(x2, mod3, w_cat, seq):
    t = x2.shape[0]
    per_b = seq // INPROJ_TM
    return pl.pallas_call(
        _inproj_kernel,
        grid=(t // INPROJ_TM,),
        in_specs=[pl.BlockSpec((INPROJ_TM, D_MODEL), lambda i: (i, 0)),
                  pl.BlockSpec((1, 1, D_MODEL), lambda i: (i // per_b, 0, 0)),
                  pl.BlockSpec((1, 1, D_MODEL), lambda i: (i // per_b, 0, 1)),
                  pl.BlockSpec((D_MODEL, D_PROJ), lambda i: (0, 0), pipeline_mode=pl.Buffered(1))],
        out_specs=pl.BlockSpec((INPROJ_TM, D_PROJ), lambda i: (i, 0)),
        out_shape=jax.ShapeDtypeStruct((t, D_PROJ), f32),
        compiler_params=pltpu.CompilerParams(vmem_limit_bytes=56 * 1024 * 1024),
        name="inproj",
    )(x2, mod3, mod3, w_cat)


CONV_TAIL = SUBLANES


def _split3(a):
    hi = a.astype(bf16)
    r1 = a - hi.astype(f32)
    mid = r1.astype(bf16)
    lo = (r1 - mid.astype(f32)).astype(bf16)
    return hi, mid, lo


def _pair_select(lane_lo, even, odd):
    return jnp.where(lane_lo, even, odd)


def _mixer_kernel(sink_ref, z_ref, xs_ref, q_ref, bc_ref, kvdt_ref, cw_ref, cb_ref, dtb_ref, alog_ref,
                  dskip_ref, gssm_ref, gatt_ref, o_ref, ext_scr, st_scr, kp_scr, vp_scr):
    c = pl.program_id(1)
    L = CHUNK

    @pl.when(c == 0)
    def _():
        ext_scr[pl.ds(0, CONV_TAIL), :] = jnp.zeros((CONV_TAIL, D_CONV), f32)
        st_scr[...] = jnp.zeros(st_scr.shape, f32)
        kp_scr[...] = jnp.zeros(kp_scr.shape, f32)
        vp_scr[...] = jnp.zeros(vp_scr.shape, f32)

    ext_scr[pl.ds(CONV_TAIL, L), pl.ds(0, D_SSM)] = xs_ref[...]
    ext_scr[pl.ds(CONV_TAIL, L), pl.ds(D_SSM, D_BC)] = bc_ref[...]
    conv = jnp.zeros((L, D_CONV), f32) + cb_ref[...]
    for w in range(CONV_WIDTH):
        off = CONV_TAIL - (CONV_WIDTH - 1) + w
        conv = conv + ext_scr[pl.ds(off, L), :] * cw_ref[pl.ds(w, 1), :]
    tail = ext_scr[pl.ds(L, CONV_TAIL), :]
    ext_scr[pl.ds(0, CONV_TAIL), :] = tail
    xbc = _silu(conv)
    xs = xbc[:, :D_SSM]

    kvdt = kvdt_ref[...]
    k_cur = kvdt[:, 0:D_KV]
    v_cur = kvdt[:, D_KV:2 * D_KV]
    dt_raw = kvdt[:, 2 * D_KV:3 * D_KV]

    xdt = dt_raw + dtb_ref[...]
    dt = jnp.maximum(xdt, 0.0) + jnp.log1p(jnp.exp(-jnp.abs(xdt)))
    a = dt * (-jnp.exp(alog_ref[...]))
    row_i = lax.broadcasted_iota(i32, (L, L), 0)
    col_i = lax.broadcasted_iota(i32, (L, L), 1)
    causal = row_i >= col_i
    tri = jnp.where(causal, 1.0, 0.0).astype(bf16)
    a_hi, a_mid, a_lo = _split3(a)
    a_cum = _dot(tri, a_hi) + _dot(tri, a_mid) + _dot(tri, a_lo)
    a_cum_t = a_cum.T
    dt_t = dt.T
    a_last = a_cum[L - 1:L, :]
    e_cum = jnp.exp(a_cum)
    to_end = jnp.exp(a_last - a_cum) * dt
    cdec = jnp.exp(a_last)
    lane_lo = lax.broadcasted_iota(i32, (L, LANES), 1) < HEADDIM
    lane_lo_row = lax.broadcasted_iota(i32, (1, LANES), 1) < HEADDIM

    y_parts = []
    for g in range(SSM_GROUPS):
        bm = xbc[:, D_SSM + g * SSM_STATE:D_SSM + (g + 1) * SSM_STATE]
        cm = xbc[:, D_SSM + (SSM_GROUPS + g) * SSM_STATE:D_SSM + (SSM_GROUPS + g + 1) * SSM_STATE]
        bm16 = bm.astype(bf16)
        cm16 = cm.astype(bf16)
        cb = _dot_nt(cm16, bm16)
        s_prev = st_scr[g]
        y_off = _dot(cm16, s_prev.astype(bf16))
        xs_scaled = []
        cd_rows = []
        for p in range(4):
            pair = g * 4 + p
            x_pair = xs[:, pair * LANES:(pair + 1) * LANES]
            x16 = x_pair.astype(bf16)
            yy = []
            for half in range(2):
                hd = 2 * pair + half
                seg = a_cum[:, hd:hd + 1] - a_cum_t[hd:hd + 1, :]
                decay = jnp.exp(jnp.where(causal, seg, NEG_INF))
                wmat = cb * decay * dt_t[hd:hd + 1, :]
                yy.append(_dot(wmat.astype(bf16), x16))
            hd0, hd1 = 2 * pair, 2 * pair + 1
            y_diag = _pair_select(lane_lo, yy[0], yy[1])
            e_pair = _pair_select(lane_lo, e_cum[:, hd0:hd0 + 1], e_cum[:, hd1:hd1 + 1])
            y_parts.append(y_diag + y_off[:, p * LANES:(p + 1) * LANES] * e_pair)
            te_pair = _pair_select(lane_lo, to_end[:, hd0:hd0 + 1], to_end[:, hd1:hd1 + 1])
            xs_scaled.append((x_pair * te_pair).astype(bf16))
            cd_rows.append(_pair_select(lane_lo_row, cdec[:, hd0:hd0 + 1], cdec[:, hd1:hd1 + 1]))
        xs_sc = jnp.concatenate(xs_scaled, axis=1)
        s_new = _dot(bm.T.astype(bf16), xs_sc)
        st_scr[g] = jnp.concatenate(cd_rows, axis=1) * s_prev + s_new
    y = jnp.concatenate(y_parts, axis=1) + dskip_ref[...] * xs
    y_ssd = _rms_norm(y * _silu(z_ref[...]), gssm_ref[...])
    o_ref[:, pl.ds(0, D_SSM)] = y_ssd.astype(o_ref.dtype)

    kcat = jnp.concatenate([kp_scr[...], k_cur], axis=0)
    vcat = jnp.concatenate([vp_scr[...], v_cur], axis=0)
    kp_scr[...] = k_cur
    vp_scr[...] = v_cur
    kroll = pltpu.roll(kcat, HEADDIM, axis=1)
    vroll = pltpu.roll(vcat, HEADDIM, axis=1)
    lane_lo2 = lax.broadcasted_iota(i32, (2 * L, LANES), 1) < HEADDIM
    qi = lax.broadcasted_iota(i32, (L, 2 * L), 0)
    kj = lax.broadcasted_iota(i32, (L, 2 * L), 1)
    dist = qi + L - kj
    valid = (dist >= 0) & (dist < L) & ((c > 0) | (kj >= L))
    att_parts = []
    for kvh in range(ATT_KV_HEADS):
        if kvh == 0:
            kd = jnp.where(lane_lo2, kcat, kroll).astype(bf16)
            vd = jnp.where(lane_lo2, vcat, vroll).astype(bf16)
        else:
            kd = jnp.where(lane_lo2, kroll, kcat).astype(bf16)
            vd = jnp.where(lane_lo2, vroll, vcat).astype(bf16)
        for p in range(4):
            pair = kvh * 4 + p
            q_pair = q_ref[:, pl.ds(pair * LANES, LANES)] * (HEADDIM ** -0.5)
            oo = []
            for half in range(2):
                hd = 2 * pair + half
                keep = lane_lo if half == 0 else jnp.logical_not(lane_lo)
                qm = jnp.where(keep, q_pair, 0.0).astype(bf16)
                s = _dot_nt(qm, kd)
                s = jnp.where(valid, s, NEG_INF)
                sink = sink_ref[hd]
                m = jnp.maximum(jnp.max(s, axis=1, keepdims=True), sink)
                pr = jnp.exp(s - m)
                denom = jnp.sum(pr, axis=1, keepdims=True) + jnp.exp(sink - m)
                oo.append(_dot((pr / denom).astype(bf16), vd))
            att_parts.append(_pair_select(lane_lo, oo[0], oo[1]))
    att = jnp.concatenate(att_parts, axis=1)
    o_ref[:, pl.ds(D_SSM, D_ATT)] = _rms_norm(att, gatt_ref[...]).astype(o_ref.dtype)


def _mixer(proj, sinks, conv_w8, conv_b, dtb, alog, dskip, gssm, gatt, batch, seq):
    t = proj.shape[0]
    nc = seq // CHUNK

    def tok(col):
        return lambda b, c: (b * nc + c, col)

    const = lambda b, c: (0, 0)
    return pl.pallas_call(
        _mixer_kernel,
        grid=(batch, nc),
        in_specs=[pl.BlockSpec(memory_space=pltpu.SMEM),
                  pl.BlockSpec((CHUNK, D_SSM), tok(0)),
                  pl.BlockSpec((CHUNK, D_SSM), tok(1)),
                  pl.BlockSpec((CHUNK, D_ATT), tok(2)),
                  pl.BlockSpec((CHUNK, D_BC), tok(6)),
                  pl.BlockSpec((CHUNK, D_BC), tok(7)),
                  pl.BlockSpec((SUBLANES, D_CONV), const),
                  pl.BlockSpec((1, D_CONV), const),
                  pl.BlockSpec((1, LANES), const),
                  pl.BlockSpec((1, LANES), const),
                  pl.BlockSpec((1, D_SSM), const),
                  pl.BlockSpec((1, D_SSM), const),
                  pl.BlockSpec((1, D_ATT), const)],
        out_specs=pl.BlockSpec((CHUNK, D_MODEL), lambda b, c: (b * nc + c, 0)),
        out_shape=jax.ShapeDtypeStruct((t, D_MODEL), bf16),
        scratch_shapes=[pltpu.VMEM((CONV_TAIL + CHUNK, D_CONV), f32),
                        pltpu.VMEM((SSM_GROUPS, SSM_STATE, D_SSM // SSM_GROUPS), f32),
                        pltpu.VMEM((CHUNK, D_KV), f32),
                        pltpu.VMEM((CHUNK, D_KV), f32)],
        compiler_params=pltpu.CompilerParams(
            dimension_semantics=("arbitrary", "arbitrary"),
            vmem_limit_bytes=40 * 1024 * 1024),
        name="mixer",
    )(sinks, proj, proj, proj, proj, proj, conv_w8, conv_b, dtb, alog, dskip, gssm, gatt)


OUT_TM = 256


PEER_UNITS = 2 * PEER_HEADS


def _outproj_kernel(y_ref, w_ref, x_ref, gate_ref, g_ref, b_ref, sh_ref, sc_ref, wq_ref,
                    x1_ref, h2_ref, q_ref):
    y = _dot(y_ref[...], w_ref[...])
    x1 = _layer_norm(ALPHA * x_ref[...] + gate_ref[0] * y, g_ref[...], b_ref[...])
    x1_ref[...] = x1
    h2 = x1 * (1.0 + sc_ref[0]) + sh_ref[0]
    h2_ref[...] = h2
    q = _dot(h2.astype(bf16), wq_ref[...])
    for u in range(PEER_UNITS):
        q_ref[u] = q[:, u * PEER_HALF:(u + 1) * PEER_HALF].astype(bf16)


def _outproj(ycat, w_out, x2, mod3, ln_g, ln_b, w_q, seq):
    t = x2.shape[0]
    per_b = seq // OUT_TM
    tok = lambda i: (i, 0)
    const = lambda i: (0, 0)

    def modspec(k):
        return pl.BlockSpec((1, 1, D_MODEL), lambda i: (i // per_b, 0, k))

    def weight():
        return pl.BlockSpec((D_MODEL, D_MODEL), const, pipeline_mode=pl.Buffered(1))

    return pl.pallas_call(
        _outproj_kernel,
        grid=(t // OUT_TM,),
        in_specs=[pl.BlockSpec((OUT_TM, D_MODEL), tok),
                  weight(),
                  pl.BlockSpec((OUT_TM, D_MODEL), tok),
                  modspec(2), pl.BlockSpec((1, D_MODEL), const), pl.BlockSpec((1, D_MODEL), const),
                  modspec(3), modspec(4),
                  weight()],
        out_specs=[pl.BlockSpec((OUT_TM, D_MODEL), tok), pl.BlockSpec((OUT_TM, D_MODEL), tok),
                   pl.BlockSpec((PEER_UNITS, OUT_TM, PEER_HALF), lambda i: (0, i, 0))],
        out_shape=[jax.ShapeDtypeStruct((t, D_MODEL), f32), jax.ShapeDtypeStruct((t, D_MODEL), f32),
                   jax.ShapeDtypeStruct((PEER_UNITS, t, PEER_HALF), bf16)],
        compiler_params=pltpu.CompilerParams(vmem_limit_bytes=48 * 1024 * 1024),
        name="outproj",
    )(ycat, w_out, x2, mod3, ln_g, ln_b, mod3, mod3, w_q)


ORDER_LAST = 2 ** 30


def _topk_rows(s, k, order, payload=None, every=None, between=None):
    vals, outs = [], []
    for it in range(k):
        if between is not None and it % every == 0:
            between()
        m = jnp.max(s, axis=0, keepdims=True)
        am = jnp.min(jnp.where(s == m, order, ORDER_LAST), axis=0, keepdims=True)
        hit = order == am
        vals.append(m)
        if payload is None:
            outs.append(am)
        else:
            outs.append(jnp.max(jnp.where(hit, payload, -1), axis=0, keepdims=True))
        s = jnp.where(hit, NEG_INF, s)
    return jnp.concatenate(vals, axis=0), jnp.concatenate(outs, axis=0)


def _candidate_slabs(v1, i1, v2, i2):
    tm = v1.shape[1]
    cands, orders, experts = [], [], []

    def slab(rows, vals, flat, eid, live):
        cands.append(jnp.where(live, vals, NEG_INF))
        orders.append(jnp.where(live, flat, ORDER_LAST))
        experts.append(eid)

    for k1, rows, n_live in ((0, 16, 16), (1, 8, 8), (2, 8, 5), (3, 8, 4)):
        k2 = lax.broadcasted_iota(i32, (rows, tm), 0)
        slab(rows, v1[k1:k1 + 1] + v2[0:rows], k1 * PEER_TOPK + k2,
             i1[k1:k1 + 1] * PEER_KEYS + i2[0:rows], k2 < n_live)
    for k2, rows, hi in ((0, 16, 16), (1, 8, 8), (2, 8, 5)):
        k1 = lax.broadcasted_iota(i32, (rows, tm), 0)
        slab(rows, v1[0:rows] + v2[k2:k2 + 1], k1 * PEER_TOPK + k2,
             i1[0:rows] * PEER_KEYS + i2[k2:k2 + 1], (k1 >= 4) & (k1 < hi))
    return jnp.concatenate(cands, axis=0), jnp.concatenate(orders, axis=0), jnp.concatenate(experts, axis=0)


ROUTE_HOOK_EVERY = 4
ROUTE_HOOK_CALLS = 2 * PEER_TOPK // ROUTE_HOOK_EVERY


def _route_head(q_ref, keys_ref, h, between=None):
    tm = q_ref.shape[1]
    key_id = lax.broadcasted_iota(i32, (PEER_KEYS, tm), 0)
    vals, idxs = [], []
    for cpart in range(2):
        unit = h * 2 + cpart
        sc = _dot_nt(keys_ref[unit], q_ref[unit])
        v, ix = _topk_rows(sc, PEER_TOPK, key_id, every=ROUTE_HOOK_EVERY, between=between)
        vals.append(v)
        idxs.append(ix)
    cand, order, cidx = _candidate_slabs(vals[0], idxs[0], vals[1], idxs[1])
    best, expert = _topk_rows(cand, PEER_TOPK, order, payload=cidx)
    ex = jnp.exp(best - jnp.max(best, axis=0, keepdims=True))
    return expert, ex / jnp.sum(ex, axis=0, keepdims=True)


EXP_TB = 128
EXP_SUB = 8
EXP_NSUB = EXP_TB // EXP_SUB
EXP_ROWS = EXP_SUB * PEER_SEL
D_UV = 2 * D_MODEL


EXP_PART = 32
EXP_PARTS = PEER_SEL // EXP_PART
N_BURSTS = EXP_SUB * EXP_PARTS
ROUTE_SUBS = PEER_HEADS
IDX_PUBLISH = ROUTE_SUBS
IDX_READY = EXP_NSUB - 2


def _peer_kernel(q_ref, keys_ref, h_ref, x1_ref, gate_ref, g_ref, b_ref, uv_hbm, o_ref,
                 buf, y_scr, e_scr, et_scr, gate_scr, idx_smem, sem, idx_sem):
    step = pl.program_id(0)
    routes = step < pl.num_programs(0) - 1
    route_buf = step % 2
    gather_buf = 1 - route_buf

    def burst(b, ids_buf, sub, slot):
        tk, part = divmod(b, EXP_PARTS)
        for k in range(part * EXP_PART, (part + 1) * EXP_PART):
            e = idx_smem[ids_buf, sub * EXP_SUB + tk, k]
            pltpu.make_async_copy(uv_hbm.at[e], buf.at[slot, pl.ds(tk * PEER_SEL + k, 1)],
                                  sem.at[slot, tk]).start()

    def wait_token(slot, tk):
        rows = pl.ds(tk * PEER_SEL, PEER_SEL)
        pltpu.make_async_copy(buf.at[slot, rows], buf.at[slot, rows], sem.at[slot, tk]).wait()

    def idx_copy():
        return pltpu.make_async_copy(et_scr, idx_smem.at[route_buf], idx_sem)

    def route_head(h, between=None):
        expert, gate = _route_head(q_ref, keys_ref, h, between=between)
        rows = pl.ds(pl.multiple_of(h * PEER_TOPK, PEER_TOPK), PEER_TOPK)
        e_scr[rows, :] = expert
        gate_scr[route_buf, rows, :] = gate

    def publish_ids():
        et_scr[...] = e_scr[...].T
        idx_copy().start()

    @pl.when(step == 0)
    def _():
        def head(h, carry):
            route_head(h)
            return carry
        lax.fori_loop(0, PEER_HEADS, head, 0)
        publish_ids()
        idx_copy().wait()
        for b in range(N_BURSTS):
            burst(b, route_buf, 0, 0)

    @pl.when(step >= 1)
    def _():
        def sub_block(j, slot):
            in_block = j + 1 < EXP_NSUB
            do_issue = in_block | routes
            ids_buf = jnp.where(in_block, gather_buf, route_buf)
            nxt = jnp.where(in_block, j + 1, 0)
            routing_sub = routes & (j < ROUTE_SUBS)

            @pl.when(routing_sub)
            def _():
                pending = iter(range(ROUTE_HOOK_CALLS))
                route_head(j, between=lambda: burst(next(pending), gather_buf, j + 1, 1 - slot))

            @pl.when(jnp.logical_not(routing_sub) & do_issue)
            def _():
                for b in range(ROUTE_HOOK_CALLS):
                    burst(b, ids_buf, nxt, 1 - slot)

            @pl.when(routes & (j == IDX_PUBLISH))
            def _():
                publish_ids()

            @pl.when(routes & (j == IDX_READY))
            def _():
                idx_copy().wait()

            gs = pltpu.roll(gate_scr[gather_buf], (EXP_TB - j * EXP_SUB) & (EXP_TB - 1), axis=1)
            per_token = (N_BURSTS - ROUTE_HOOK_CALLS) // EXP_SUB
            for tk in range(EXP_SUB):
                @pl.when(do_issue)
                def _():
                    for b in range(ROUTE_HOOK_CALLS + tk * per_token, ROUTE_HOOK_CALLS + (tk + 1) * per_token):
                        burst(b, ids_buf, nxt, 1 - slot)

                wait_token(slot, tk)
                rows = pl.ds(tk * PEER_SEL, PEER_SEL)
                tok_row = pl.ds(j * EXP_SUB + tk, 1)
                a = jnp.sum(buf[slot, rows, pl.ds(0, D_MODEL)] * h_ref[tok_row, :], axis=1, keepdims=True)
                wgt = gs[:, tk:tk + 1] * _gelu(a)
                y_scr[tok_row, :] = jnp.sum(buf[slot, rows, pl.ds(D_MODEL, D_MODEL)] * wgt, axis=0,
                                            keepdims=True)

        def sub_block_pair(jj, carry):
            sub_block(2 * jj, 0)
            sub_block(2 * jj + 1, 1)
            return carry

        lax.fori_loop(0, EXP_NSUB // 2, sub_block_pair, 0)
        o_ref[...] = _layer_norm(ALPHA * x1_ref[...] + gate_ref[0] * y_scr[...], g_ref[...], b_ref[...])


def _peer(q_units, keys, h2, x1, mod3, ln_g, ln_b, uv_tab, seq):
    t = h2.shape[0]
    n_blocks = t // EXP_TB
    per_b = seq // EXP_TB
    assert EXP_NSUB % 2 == 0, "the cross-step prefetch assumes every grid step starts on slot 0"
    assert ROUTE_SUBS <= IDX_PUBLISH < IDX_READY < EXP_NSUB - 1
    gathered = lambda s: jnp.maximum(s - 1, 0)
    tok = lambda s: (gathered(s), 0)
    const = lambda s: (0, 0)
    return pl.pallas_call(
        _peer_kernel,
        grid=(n_blocks + 1,),
        in_specs=[pl.BlockSpec((PEER_UNITS, EXP_TB, PEER_HALF), lambda s: (0, jnp.minimum(s, n_blocks - 1), 0)),
                  pl.BlockSpec((PEER_UNITS, PEER_KEYS, PEER_HALF), lambda s: (0, 0, 0)),
                  pl.BlockSpec((EXP_TB, D_MODEL), tok),
                  pl.BlockSpec((EXP_TB, D_MODEL), tok),
                  pl.BlockSpec((1, 1, D_MODEL), lambda s: (gathered(s) // per_b, 0, 5)),
                  pl.BlockSpec((1, D_MODEL), const),
                  pl.BlockSpec((1, D_MODEL), const),
                  pl.BlockSpec(memory_space=pl.ANY)],
        out_specs=pl.BlockSpec((EXP_TB, D_MODEL), tok),
        out_shape=jax.ShapeDtypeStruct((t, D_MODEL), f32),
        scratch_shapes=[pltpu.VMEM((2, EXP_ROWS, D_UV), f32),
                        pltpu.VMEM((EXP_TB, D_MODEL), f32),
                        pltpu.VMEM((PEER_SEL, EXP_TB), i32),
                        pltpu.VMEM((EXP_TB, PEER_SEL), i32),
                        pltpu.VMEM((2, PEER_SEL, EXP_TB), f32),
                        pltpu.SMEM((2, EXP_TB, PEER_SEL), i32),
                        pltpu.SemaphoreType.DMA((2, EXP_SUB)),
                        pltpu.SemaphoreType.DMA(())],
        compiler_params=pltpu.CompilerParams(
            dimension_semantics=("arbitrary",),
            vmem_limit_bytes=56 * 1024 * 1024),
        name="peer",
    )(q_units, keys, h2, x1, mod3, ln_g, ln_b, uv_tab)


def _regroup_w_in(w):
    o_xbc = D_SSM
    o_dt = o_xbc + D_CONV
    o_q = o_dt + SSM_HEADS
    o_k = o_q + D_ATT
    o_v = o_k + D_KV
    pad = jnp.zeros((w.shape[0], D_PROJ - (2 * D_SSM + D_ATT + D_BC + 2 * D_KV + SSM_HEADS)), w.dtype)
    return jnp.concatenate([w[:, :D_SSM], w[:, o_xbc:o_xbc + D_SSM], w[:, o_q:o_k],
                            w[:, o_xbc + D_SSM:o_dt], w[:, o_k:o_v], w[:, o_v:o_v + D_KV],
                            w[:, o_dt:o_q], pad], axis=1)


def _pad_lanes(v):
    return jnp.pad(v, (0, LANES - v.shape[0]))[None, :]


def kernel(x, c, w_ada, b_ada, w_in, conv_w, conv_b, dt_bias, a_log, d_skip, ssm_norm_g, attn_sinks, attn_norm_g, w_out, ln1_g, ln1_b, peer_w_q, peer_sub_keys, peer_u, peer_v, ln2_g, ln2_b):
    batch, seq, d = x.shape
    t = batch * seq
    x2 = x.reshape(t, d)
    for l in range(DEPTH):
        c_pad = jnp.pad(c, ((0, SUBLANES - batch), (0, 0)))
        mod = _ada(c_pad, w_ada[l], b_ada[l][None, :])
        mod3 = mod[:batch].reshape(batch, 1, N_MOD * d)
        proj = _inproj(x2, mod3, _regroup_w_in(w_in[l]).astype(bf16), seq)
        conv_w8 = jnp.pad(conv_w[l], ((0, SUBLANES - CONV_WIDTH), (0, 0)))
        ycat = _mixer(proj, attn_sinks[l], conv_w8, conv_b[l][None, :], _pad_lanes(dt_bias[l]),
                      _pad_lanes(a_log[l]), jnp.repeat(d_skip[l], HEADDIM)[None, :],
                      ssm_norm_g[l][None, :], attn_norm_g[l][None, :], batch, seq)
        x1, h2, q_units = _outproj(ycat, w_out[l].astype(bf16), x2, mod3, ln1_g[l][None, :], ln1_b[l][None, :],
                                   peer_w_q[l].astype(bf16), seq)
        keys = peer_sub_keys[l].astype(bf16).reshape(PEER_UNITS, PEER_KEYS, PEER_HALF)
        uv_tab = jnp.concatenate([peer_u[l][:, None, :], peer_v[l][:, None, :]], axis=2)
        x2 = _peer(q_units, keys, h2, x1, mod3, ln2_g[l][None, :], ln2_b[l][None, :], uv_tab, seq)
    return x2.reshape(batch, seq, d)
```

```python
import jax
import jax.numpy as jnp
from jax import lax
from jax.experimental import pallas as pl
from jax.experimental.pallas import tpu as pltpu

f32 = jnp.float32
bf16 = jnp.bfloat16
i32 = jnp.int32

D_MODEL = 2048
D_SSM = 1024
D_ATT = 1024
SSM_HEADS = 16
SSM_GROUPS = 2
SSM_STATE = 128
HEADDIM = 64
CONV_WIDTH = 4
CHUNK = 128
D_BC = 2 * SSM_GROUPS * SSM_STATE
D_CONV = D_SSM + D_BC
ATT_HEADS = 16
ATT_KV_HEADS = 2
D_KV = ATT_KV_HEADS * HEADDIM
PEER_HEADS = 8
PEER_KEYS = 128
PEER_TOPK = 16
PEER_HALF = 128
PEER_SEL = PEER_HEADS * PEER_TOPK
N_MOD = 6
DEPTH = 1
ALPHA = (2.0 * DEPTH) ** 0.25
EPS = 1e-5

LANES = 128
SUBLANES = 8
VMEM_LIMIT_BYTES = 56 * 1024 * 1024
D_PROJ = 4096
NEG_INF = float("-inf")


def _silu(x):
    return x * jax.nn.sigmoid(x)


def _gelu(x):
    return 0.5 * x * (1.0 + lax.erf(x * (2.0 ** -0.5)))


def _dot(a, b):
    return jnp.dot(a, b, preferred_element_type=f32)


def _dot_nt(a, b):
    return lax.dot_general(a, b, (((1,), (1,)), ((), ())), preferred_element_type=f32)


def _layer_norm(x, g, b):
    mu = jnp.mean(x, axis=-1, keepdims=True)
    xc = x - mu
    var = jnp.mean(xc * xc, axis=-1, keepdims=True)
    return xc * lax.rsqrt(var + EPS) * g + b


def _rms_norm(x, g):
    return x * lax.rsqrt(jnp.mean(x * x, axis=-1, keepdims=True) + EPS) * g


ADA_TN = 1024


def _ada_kernel(c_ref, w_ref, b_ref, o_ref):
    sc = _silu(c_ref[...])
    o_ref[...] = _dot(sc.astype(bf16), w_ref[...].astype(bf16)) + b_ref[...]


def _ada(c_pad, w_ada, b_ada):
    n = w_ada.shape[1]
    return pl.pallas_call(
        _ada_kernel,
        grid=(n // ADA_TN,),
        in_specs=[pl.BlockSpec((SUBLANES, D_MODEL), lambda j: (0, 0)),
                  pl.BlockSpec((D_MODEL, ADA_TN), lambda j: (0, j)),
                  pl.BlockSpec((1, ADA_TN), lambda j: (0, j))],
        out_specs=pl.BlockSpec((SUBLANES, ADA_TN), lambda j: (0, j)),
        out_shape=jax.ShapeDtypeStruct((SUBLANES, n), f32),
        compiler_params=pltpu.CompilerParams(vmem_limit_bytes=VMEM_LIMIT_BYTES),
        name="ada",
    )(c_pad, w_ada, b_ada)


INPROJ_TM = 512
INPROJ_TN = 1024


def _inproj_kernel(x_ref, sh_ref, sc_ref, w_ref, o_ref):
    h = (x_ref[...] * (1.0 + sc_ref[0]) + sh_ref[0]).astype(bf16)
    for n in range(D_PROJ // INPROJ_TN):
        cols = pl.ds(n * INPROJ_TN, INPROJ_TN)
        o_ref[:, cols] = _dot(h, w_ref[:, cols])


def _inproj(x2, mod3, w_cat, seq):
    t = x2.shape[0]
    per_b = seq // INPROJ_TM
    return pl.pallas_call(
        _inproj_kernel,
        grid=(t // INPROJ_TM,),
        in_specs=[pl.BlockSpec((INPROJ_TM, D_MODEL), lambda i: (i, 0)),
                  pl.BlockSpec((1, 1, D_MODEL), lambda i: (i // per_b, 0, 0)),
                  pl.BlockSpec((1, 1, D_MODEL), lambda i: (i // per_b, 0, 1)),
                  pl.BlockSpec((D_MODEL, D_PROJ), lambda i: (0, 0), pipeline_mode=pl.Buffered(1))],
        out_specs=pl.BlockSpec((INPROJ_TM, D_PROJ), lambda i: (i, 0)),
        out_shape=jax.ShapeDtypeStruct((t, D_PROJ), f32),
        compiler_params=pltpu.CompilerParams(vmem_limit_bytes=VMEM_LIMIT_BYTES),
        name="inproj",
    )(x2, mod3, mod3, w_cat)


CONV_TAIL = SUBLANES


def _split3(a):
    hi = a.astype(bf16)
    r1 = a - hi.astype(f32)
    mid = r1.astype(bf16)
    lo = (r1 - mid.astype(f32)).astype(bf16)
    return hi, mid, lo


def _pair_select(lane_lo, even, odd):
    return jnp.where(lane_lo, even, odd)


def _mixer_kernel(sink_ref, z_ref, xs_ref, q_ref, bc_ref, kvdt_ref, cw_ref, cb_ref, dtb_ref, alog_ref,
                  dskip_ref, gssm_ref, gatt_ref, o_ref, ext_scr, st_scr, kp_scr, vp_scr):
    c = pl.program_id(1)
    L = CHUNK

    @pl.when(c == 0)
    def _():
        ext_scr[pl.ds(0, CONV_TAIL), :] = jnp.zeros((CONV_TAIL, D_CONV), f32)
        st_scr[...] = jnp.zeros(st_scr.shape, f32)
        kp_scr[...] = jnp.zeros(kp_scr.shape, f32)
        vp_scr[...] = jnp.zeros(vp_scr.shape, f32)

    ext_scr[pl.ds(CONV_TAIL, L), pl.ds(0, D_SSM)] = xs_ref[...]
    ext_scr[pl.ds(CONV_TAIL, L), pl.ds(D_SSM, D_BC)] = bc_ref[...]
    conv = jnp.zeros((L, D_CONV), f32) + cb_ref[...]
    for w in range(CONV_WIDTH):
        off = CONV_TAIL - (CONV_WIDTH - 1) + w
        conv = conv + ext_scr[pl.ds(off, L), :] * cw_ref[pl.ds(w, 1), :]
    tail = ext_scr[pl.ds(L, CONV_TAIL), :]
    ext_scr[pl.ds(0, CONV_TAIL), :] = tail
    xbc = _silu(conv)
    xs = xbc[:, :D_SSM]

    kvdt = kvdt_ref[...]
    k_cur = kvdt[:, 0:D_KV]
    v_cur = kvdt[:, D_KV:2 * D_KV]
    dt_raw = kvdt[:, 2 * D_KV:3 * D_KV]

    xdt = dt_raw + dtb_ref[...]
    dt = jnp.maximum(xdt, 0.0) + jnp.log1p(jnp.exp(-jnp.abs(xdt)))
    a = dt * (-jnp.exp(alog_ref[...]))
    row_i = lax.broadcasted_iota(i32, (L, L), 0)
    col_i = lax.broadcasted_iota(i32, (L, L), 1)
    causal = row_i >= col_i
    tri = jnp.where(causal, 1.0, 0.0).astype(bf16)
    a_hi, a_mid, a_lo = _split3(a)
    a_cum = _dot(tri, a_hi) + _dot(tri, a_mid) + _dot(tri, a_lo)
    a_cum_t = a_cum.T
    dt_t = dt.T
    a_last = a_cum[L - 1:L, :]
    e_cum = jnp.exp(a_cum)
    to_end = jnp.exp(a_last - a_cum) * dt
    cdec = jnp.exp(a_last)
    lane_lo = lax.broadcasted_iota(i32, (L, LANES), 1) < HEADDIM
    lane_lo_row = lax.broadcasted_iota(i32, (1, LANES), 1) < HEADDIM

    y_parts = []
    for g in range(SSM_GROUPS):
        bm = xbc[:, D_SSM + g * SSM_STATE:D_SSM + (g + 1) * SSM_STATE]
        cm = xbc[:, D_SSM + (SSM_GROUPS + g) * SSM_STATE:D_SSM + (SSM_GROUPS + g + 1) * SSM_STATE]
        bm16 = bm.astype(bf16)
        cm16 = cm.astype(bf16)
        cb = _dot_nt(cm16, bm16)
        s_prev = st_scr[g]
        y_off = _dot(cm16, s_prev.astype(bf16))
        xs_scaled = []
        cd_rows = []
        for p in range(4):
            pair = g * 4 + p
            x_pair = xs[:, pair * LANES:(pair + 1) * LANES]
            x16 = x_pair.astype(bf16)
            yy = []
            for half in range(2):
                hd = 2 * pair + half
                seg = a_cum[:, hd:hd + 1] - a_cum_t[hd:hd + 1, :]
                decay = jnp.exp(jnp.where(causal, seg, NEG_INF))
                wmat = cb * decay * dt_t[hd:hd + 1, :]
                yy.append(_dot(wmat.astype(bf16), x16))
            hd0, hd1 = 2 * pair, 2 * pair + 1
            y_diag = _pair_select(lane_lo, yy[0], yy[1])
            e_pair = _pair_select(lane_lo, e_cum[:, hd0:hd0 + 1], e_cum[:, hd1:hd1 + 1])
            y_parts.append(y_diag + y_off[:, p * LANES:(p + 1) * LANES] * e_pair)
            te_pair = _pair_select(lane_lo, to_end[:, hd0:hd0 + 1], to_end[:, hd1:hd1 + 1])
            xs_scaled.append((x_pair * te_pair).astype(bf16))
            cd_rows.append(_pair_select(lane_lo_row, cdec[:, hd0:hd0 + 1], cdec[:, hd1:hd1 + 1]))
        xs_sc = jnp.concatenate(xs_scaled, axis=1)
        s_new = _dot(bm.T.astype(bf16), xs_sc)
        st_scr[g] = jnp.concatenate(cd_rows, axis=1) * s_prev + s_new
    y = jnp.concatenate(y_parts, axis=1) + dskip_ref[...] * xs
    y_ssd = _rms_norm(y * _silu(z_ref[...]), gssm_ref[...])
    o_ref[:, pl.ds(0, D_SSM)] = y_ssd.astype(o_ref.dtype)

    kcat = jnp.concatenate([kp_scr[...], k_cur], axis=0)
    vcat = jnp.concatenate([vp_scr[...], v_cur], axis=0)
    kp_scr[...] = k_cur
    vp_scr[...] = v_cur
    kroll = pltpu.roll(kcat, HEADDIM, axis=1)
    vroll = pltpu.roll(vcat, HEADDIM, axis=1)
    lane_lo2 = lax.broadcasted_iota(i32, (2 * L, LANES), 1) < HEADDIM
    qi = lax.broadcasted_iota(i32, (L, 2 * L), 0)
    kj = lax.broadcasted_iota(i32, (L, 2 * L), 1)
    dist = qi + L - kj
    valid = (dist >= 0) & (dist < L) & ((c > 0) | (kj >= L))
    att_parts = []
    for kvh in range(ATT_KV_HEADS):
        if kvh == 0:
            kd = jnp.where(lane_lo2, kcat, kroll).astype(bf16)
            vd = jnp.where(lane_lo2, vcat, vroll).astype(bf16)
        else:
            kd = jnp.where(lane_lo2, kroll, kcat).astype(bf16)
            vd = jnp.where(lane_lo2, vroll, vcat).astype(bf16)
        for p in range(4):
            pair = kvh * 4 + p
            q_pair = q_ref[:, pl.ds(pair * LANES, LANES)] * (HEADDIM ** -0.5)
            oo = []
            for half in range(2):
                hd = 2 * pair + half
                keep = lane_lo if half == 0 else jnp.logical_not(lane_lo)
                qm = jnp.where(keep, q_pair, 0.0).astype(bf16)
                s = _dot_nt(qm, kd)
                s = jnp.where(valid, s, NEG_INF)
                sink = sink_ref[hd]
                m = jnp.maximum(jnp.max(s, axis=1, keepdims=True), sink)
                pr = jnp.exp(s - m)
                denom = jnp.sum(pr, axis=1, keepdims=True) + jnp.exp(sink - m)
                oo.append(_dot((pr / denom).astype(bf16), vd))
            att_parts.append(_pair_select(lane_lo, oo[0], oo[1]))
    att = jnp.concatenate(att_parts, axis=1)
    o_ref[:, pl.ds(D_SSM, D_ATT)] = _rms_norm(att, gatt_ref[...]).astype(o_ref.dtype)


def _mixer(proj, sinks, conv_w8, conv_b, dtb, alog, dskip, gssm, gatt, batch, seq):
    t = proj.shape[0]
    nc = seq // CHUNK

    def tok(col):
        return lambda b, c: (b * nc + c, col)

    const = lambda b, c: (0, 0)
    return pl.pallas_call(
        _mixer_kernel,
        grid=(batch, nc),
        in_specs=[pl.BlockSpec(memory_space=pltpu.SMEM),
                  pl.BlockSpec((CHUNK, D_SSM), tok(0)),
                  pl.BlockSpec((CHUNK, D_SSM), tok(1)),
                  pl.BlockSpec((CHUNK, D_ATT), tok(2)),
                  pl.BlockSpec((CHUNK, D_BC), tok(6)),
                  pl.BlockSpec((CHUNK, D_BC), tok(7)),
                  pl.BlockSpec((SUBLANES, D_CONV), const),
                  pl.BlockSpec((1, D_CONV), const),
                  pl.BlockSpec((1, LANES), const),
                  pl.BlockSpec((1, LANES), const),
                  pl.BlockSpec((1, D_SSM), const),
                  pl.BlockSpec((1, D_SSM), const),
                  pl.BlockSpec((1, D_ATT), const)],
        out_specs=pl.BlockSpec((CHUNK, D_MODEL), lambda b, c: (b * nc + c, 0)),
        out_shape=jax.ShapeDtypeStruct((t, D_MODEL), bf16),
        scratch_shapes=[pltpu.VMEM((CONV_TAIL + CHUNK, D_CONV), f32),
                        pltpu.VMEM((SSM_GROUPS, SSM_STATE, D_SSM // SSM_GROUPS), f32),
                        pltpu.VMEM((CHUNK, D_KV), f32),
                        pltpu.VMEM((CHUNK, D_KV), f32)],
        compiler_params=pltpu.CompilerParams(
            dimension_semantics=("arbitrary", "arbitrary"),
            vmem_limit_bytes=VMEM_LIMIT_BYTES),
        name="mixer",
    )(sinks, proj, proj, proj, proj, proj, conv_w8, conv_b, dtb, alog, dskip, gssm, gatt)


OUT_TM = 256


PEER_UNITS = 2 * PEER_HEADS


def _outproj_kernel(y_ref, w_ref, x_ref, gate_ref, g_ref, b_ref, sh_ref, sc_ref, wq_ref,
                    x1_ref, h2_ref, q_ref):
    y = _dot(y_ref[...], w_ref[...])
    x1 = _layer_norm(ALPHA * x_ref[...] + gate_ref[0] * y, g_ref[...], b_ref[...])
    x1_ref[...] = x1
    h2 = x1 * (1.0 + sc_ref[0]) + sh_ref[0]
    h2_ref[...] = h2
    q = _dot(h2.astype(bf16), wq_ref[...])
    for u in range(PEER_UNITS):
        q_ref[u] = q[:, u * PEER_HALF:(u + 1) * PEER_HALF].astype(bf16)


def _outproj(ycat, w_out, x2, mod3, ln_g, ln_b, w_q, seq):
    t = x2.shape[0]
    per_b = seq // OUT_TM
    tok = lambda i: (i, 0)
    const = lambda i: (0, 0)

    def modspec(k):
        return pl.BlockSpec((1, 1, D_MODEL), lambda i: (i // per_b, 0, k))

    def weight():
        return pl.BlockSpec((D_MODEL, D_MODEL), const, pipeline_mode=pl.Buffered(1))

    return pl.pallas_call(
        _outproj_kernel,
        grid=(t // OUT_TM,),
        in_specs=[pl.BlockSpec((OUT_TM, D_MODEL), tok),
                  weight(),
                  pl.BlockSpec((OUT_TM, D_MODEL), tok),
                  modspec(2), pl.BlockSpec((1, D_MODEL), const), pl.BlockSpec((1, D_MODEL), const),
                  modspec(3), modspec(4),
                  weight()],
        out_specs=[pl.BlockSpec((OUT_TM, D_MODEL), tok), pl.BlockSpec((OUT_TM, D_MODEL), tok),
                   pl.BlockSpec((PEER_UNITS, OUT_TM, PEER_HALF), lambda i: (0, i, 0))],
        out_shape=[jax.ShapeDtypeStruct((t, D_MODEL), f32), jax.ShapeDtypeStruct((t, D_MODEL), f32),
                   jax.ShapeDtypeStruct((PEER_UNITS, t, PEER_HALF), bf16)],
        compiler_params=pltpu.CompilerParams(vmem_limit_bytes=VMEM_LIMIT_BYTES),
        name="outproj",
    )(ycat, w_out, x2, mod3, ln_g, ln_b, mod3, mod3, w_q)


ORDER_LAST = 2 ** 30


def _topk_rows(s, k, order, payload=None, every=None, between=None):
    vals, outs = [], []
    for it in range(k):
        if between is not None and it % every == 0:
            between()
        m = jnp.max(s, axis=0, keepdims=True)
        am = jnp.min(jnp.where(s == m, order, ORDER_LAST), axis=0, keepdims=True)
        hit = order == am
        vals.append(m)
        if payload is None:
            outs.append(am)
        else:
            outs.append(jnp.max(jnp.where(hit, payload, -1), axis=0, keepdims=True))
        s = jnp.where(hit, NEG_INF, s)
    return jnp.concatenate(vals, axis=0), jnp.concatenate(outs, axis=0)


def _candidate_slabs(v1, i1, v2, i2):
    tm = v1.shape[1]
    cands, orders, experts = [], [], []

    def slab(vals, flat, eid, live):
        cands.append(jnp.where(live, vals, NEG_INF))
        orders.append(jnp.where(live, flat, ORDER_LAST))
        experts.append(eid)

    for k1, rows, n_live in ((0, 16, 16), (1, 8, 8), (2, 8, 5), (3, 8, 4)):
        k2 = lax.broadcasted_iota(i32, (rows, tm), 0)
        slab(v1[k1:k1 + 1] + v2[0:rows], k1 * PEER_TOPK + k2,
             i1[k1:k1 + 1] * PEER_KEYS + i2[0:rows], k2 < n_live)
    for k2, rows, hi in ((0, 16, 16), (1, 8, 8), (2, 8, 5)):
        k1 = lax.broadcasted_iota(i32, (rows, tm), 0)
        slab(v1[0:rows] + v2[k2:k2 + 1], k1 * PEER_TOPK + k2,
             i1[0:rows] * PEER_KEYS + i2[k2:k2 + 1], (k1 >= 4) & (k1 < hi))
    return jnp.concatenate(cands, axis=0), jnp.concatenate(orders, axis=0), jnp.concatenate(experts, axis=0)


ROUTE_HOOK_EVERY = 4
ROUTE_HOOK_CALLS = 2 * PEER_TOPK // ROUTE_HOOK_EVERY


def _route_head(q_ref, keys_ref, h, between=None):
    tm = q_ref.shape[1]
    key_id = lax.broadcasted_iota(i32, (PEER_KEYS, tm), 0)
    vals, idxs = [], []
    for cpart in range(2):
        unit = h * 2 + cpart
        sc = _dot_nt(keys_ref[unit], q_ref[unit])
        v, ix = _topk_rows(sc, PEER_TOPK, key_id, every=ROUTE_HOOK_EVERY, between=between)
        vals.append(v)
        idxs.append(ix)
    cand, order, cidx = _candidate_slabs(vals[0], idxs[0], vals[1], idxs[1])
    best, expert = _topk_rows(cand, PEER_TOPK, order, payload=cidx)
    ex = jnp.exp(best - jnp.max(best, axis=0, keepdims=True))
    return expert, ex / jnp.sum(ex, axis=0, keepdims=True)


EXP_TB = 128
EXP_SUB = 8
EXP_NSUB = EXP_TB // EXP_SUB
EXP_ROWS = EXP_SUB * PEER_SEL
D_UV = 2 * D_MODEL


EXP_PART = 32
EXP_PARTS = PEER_SEL // EXP_PART
N_BURSTS = EXP_SUB * EXP_PARTS
ROUTE_SUBS = PEER_HEADS
IDX_PUBLISH = ROUTE_SUBS
IDX_READY = EXP_NSUB - 2


def _peer_kernel(q_ref, keys_ref, h_ref, x1_ref, gate_ref, g_ref, b_ref, uv_hbm, o_ref,
                 buf, y_scr, e_scr, et_scr, gate_scr, idx_smem, sem, idx_sem):
    step = pl.program_id(0)
    routes = step < pl.num_programs(0) - 1
    route_buf = step % 2
    gather_buf = 1 - route_buf

    def burst(b, ids_buf, sub, slot):
        tk, part = divmod(b, EXP_PARTS)
        for k in range(part * EXP_PART, (part + 1) * EXP_PART):
            e = idx_smem[ids_buf, sub * EXP_SUB + tk, k]
            pltpu.make_async_copy(uv_hbm.at[e], buf.at[slot, pl.ds(tk * PEER_SEL + k, 1)],
                                  sem.at[slot, tk]).start()

    def wait_token(slot, tk):
        rows = pl.ds(tk * PEER_SEL, PEER_SEL)
        pltpu.make_async_copy(buf.at[slot, rows], buf.at[slot, rows], sem.at[slot, tk]).wait()

    def idx_copy():
        return pltpu.make_async_copy(et_scr, idx_smem.at[route_buf], idx_sem)

    def route_head(h, between=None):
        expert, gate = _route_head(q_ref, keys_ref, h, between=between)
        rows = pl.ds(pl.multiple_of(h * PEER_TOPK, PEER_TOPK), PEER_TOPK)
        e_scr[rows, :] = expert
        gate_scr[route_buf, rows, :] = gate

    def publish_ids():
        et_scr[...] = e_scr[...].T
        idx_copy().start()

    @pl.when(step == 0)
    def _():
        def head(h, carry):
            route_head(h)
            return carry
        lax.fori_loop(0, PEER_HEADS, head, 0)
        publish_ids()
        idx_copy().wait()
        for b in range(N_BURSTS):
            burst(b, route_buf, 0, 0)

    @pl.when(step >= 1)
    def _():
        def sub_block(j, slot):
            in_block = j + 1 < EXP_NSUB
            do_issue = in_block | routes
            ids_buf = jnp.where(in_block, gather_buf, route_buf)
            nxt = jnp.where(in_block, j + 1, 0)
            routing_sub = routes & (j < ROUTE_SUBS)

            @pl.when(routing_sub)
            def _():
                pending = iter(range(ROUTE_HOOK_CALLS))
                route_head(j, between=lambda: burst(next(pending), gather_buf, j + 1, 1 - slot))

            @pl.when(jnp.logical_not(routing_sub) & do_issue)
            def _():
                for b in range(ROUTE_HOOK_CALLS):
                    burst(b, ids_buf, nxt, 1 - slot)

            @pl.when(routes & (j == IDX_PUBLISH))
            def _():
                publish_ids()

            @pl.when(routes & (j == IDX_READY))
            def _():
                idx_copy().wait()

            gs = pltpu.roll(gate_scr[gather_buf], (EXP_TB - j * EXP_SUB) & (EXP_TB - 1), axis=1)
            per_token = (N_BURSTS - ROUTE_HOOK_CALLS) // EXP_SUB
            for tk in range(EXP_SUB):
                @pl.when(do_issue)
                def _():
                    for b in range(ROUTE_HOOK_CALLS + tk * per_token, ROUTE_HOOK_CALLS + (tk + 1) * per_token):
                        burst(b, ids_buf, nxt, 1 - slot)

                wait_token(slot, tk)
                rows = pl.ds(tk * PEER_SEL, PEER_SEL)
                tok_row = pl.ds(j * EXP_SUB + tk, 1)
                a = jnp.sum(buf[slot, rows, pl.ds(0, D_MODEL)] * h_ref[tok_row, :], axis=1, keepdims=True)
                wgt = gs[:, tk:tk + 1] * _gelu(a)
                y_scr[tok_row, :] = jnp.sum(buf[slot, rows, pl.ds(D_MODEL, D_MODEL)] * wgt, axis=0,
                                            keepdims=True)

        def sub_block_pair(jj, carry):
            sub_block(2 * jj, 0)
            sub_block(2 * jj + 1, 1)
            return carry

        lax.fori_loop(0, EXP_NSUB // 2, sub_block_pair, 0)
        o_ref[...] = _layer_norm(ALPHA * x1_ref[...] + gate_ref[0] * y_scr[...], g_ref[...], b_ref[...])


def _peer(q_units, keys, h2, x1, mod3, ln_g, ln_b, uv_tab, seq):
    t = h2.shape[0]
    n_blocks = t // EXP_TB
    per_b = seq // EXP_TB
    assert EXP_NSUB % 2 == 0, "the cross-step prefetch assumes every grid step starts on slot 0"
    assert ROUTE_SUBS <= IDX_PUBLISH < IDX_READY < EXP_NSUB - 1
    gathered = lambda s: jnp.maximum(s - 1, 0)
    tok = lambda s: (gathered(s), 0)
    const = lambda s: (0, 0)
    return pl.pallas_call(
        _peer_kernel,
        grid=(n_blocks + 1,),
        in_specs=[pl.BlockSpec((PEER_UNITS, EXP_TB, PEER_HALF), lambda s: (0, jnp.minimum(s, n_blocks - 1), 0)),
                  pl.BlockSpec((PEER_UNITS, PEER_KEYS, PEER_HALF), lambda s: (0, 0, 0)),
                  pl.BlockSpec((EXP_TB, D_MODEL), tok),
                  pl.BlockSpec((EXP_TB, D_MODEL), tok),
                  pl.BlockSpec((1, 1, D_MODEL), lambda s: (gathered(s) // per_b, 0, 5)),
                  pl.BlockSpec((1, D_MODEL), const),
                  pl.BlockSpec((1, D_MODEL), const),
                  pl.BlockSpec(memory_space=pl.ANY)],
        out_specs=pl.BlockSpec((EXP_TB, D_MODEL), tok),
        out_shape=jax.ShapeDtypeStruct((t, D_MODEL), f32),
        scratch_shapes=[pltpu.VMEM((2, EXP_ROWS, D_UV), f32),
                        pltpu.VMEM((EXP_TB, D_MODEL), f32),
                        pltpu.VMEM((PEER_SEL, EXP_TB), i32),
                        pltpu.VMEM((EXP_TB, PEER_SEL), i32),
                        pltpu.VMEM((2, PEER_SEL, EXP_TB), f32),
                        pltpu.SMEM((2, EXP_TB, PEER_SEL), i32),
                        pltpu.SemaphoreType.DMA((2, EXP_SUB)),
                        pltpu.SemaphoreType.DMA(())],
        compiler_params=pltpu.CompilerParams(
            dimension_semantics=("arbitrary",),
            vmem_limit_bytes=VMEM_LIMIT_BYTES),
        name="peer",
    )(q_units, keys, h2, x1, mod3, ln_g, ln_b, uv_tab)


def _regroup_w_in(w):
    o_xbc = D_SSM
    o_dt = o_xbc + D_CONV
    o_q = o_dt + SSM_HEADS
    o_k = o_q + D_ATT
    o_v = o_k + D_KV
    pad = jnp.zeros((w.shape[0], D_PROJ - (2 * D_SSM + D_ATT + D_BC + 2 * D_KV + SSM_HEADS)), w.dtype)
    return jnp.concatenate([w[:, :D_SSM], w[:, o_xbc:o_xbc + D_SSM], w[:, o_q:o_k],
                            w[:, o_xbc + D_SSM:o_dt], w[:, o_k:o_v], w[:, o_v:o_v + D_KV],
                            w[:, o_dt:o_q], pad], axis=1)


def _pad_lanes(v):
    return jnp.pad(v, (0, LANES - v.shape[0]))[None, :]


def kernel(x, c, w_ada, b_ada, w_in, conv_w, conv_b, dt_bias, a_log, d_skip, ssm_norm_g, attn_sinks, attn_norm_g, w_out, ln1_g, ln1_b, peer_w_q, peer_sub_keys, peer_u, peer_v, ln2_g, ln2_b):
    batch, seq, d = x.shape
    t = batch * seq
    x2 = x.reshape(t, d)
    for l in range(DEPTH):
        c_pad = jnp.pad(c, ((0, SUBLANES - batch), (0, 0)))
        mod = _ada(c_pad, w_ada[l], b_ada[l][None, :])
        mod3 = mod[:batch].reshape(batch, 1, N_MOD * d)
        proj = _inproj(x2, mod3, _regroup_w_in(w_in[l].astype(bf16)), seq)
        conv_w8 = jnp.pad(conv_w[l], ((0, SUBLANES - CONV_WIDTH), (0, 0)))
        ycat = _mixer(proj, attn_sinks[l], conv_w8, conv_b[l][None, :], _pad_lanes(dt_bias[l]),
                      _pad_lanes(a_log[l]), jnp.repeat(d_skip[l], HEADDIM)[None, :],
                      ssm_norm_g[l][None, :], attn_norm_g[l][None, :], batch, seq)
        x1, h2, q_units = _outproj(ycat, w_out[l].astype(bf16), x2, mod3, ln1_g[l][None, :], ln1_b[l][None, :],
                                   peer_w_q[l].astype(bf16), seq)
        keys = peer_sub_keys[l].astype(bf16).reshape(PEER_UNITS, PEER_KEYS, PEER_HALF)
        uv_tab = jnp.concatenate([peer_u[l], peer_v[l]], axis=1)[:, None, :]
        x2 = _peer(q_units, keys, h2, x1, mod3, ln2_g[l][None, :], ln2_b[l][None, :], uv_tab, seq)
    return x2.reshape(batch, seq, d)
```

```python
import jax
import jax.numpy as jnp
from jax import lax
from jax.experimental import pallas as pl
from jax.experimental.pallas import tpu as pltpu

f32 = jnp.float32
bf16 = jnp.bfloat16
i32 = jnp.int32

D_MODEL = 2048
D_SSM = 1024
D_ATT = 1024
SSM_HEADS = 16
SSM_GROUPS = 2
SSM_STATE = 128
HEADDIM = 64
CONV_WIDTH = 4
CHUNK = 128
D_BC = 2 * SSM_GROUPS * SSM_STATE
D_CONV = D_SSM + D_BC
ATT_HEADS = 16
ATT_KV_HEADS = 2
D_KV = ATT_KV_HEADS * HEADDIM
PEER_HEADS = 8
PEER_KEYS = 128
PEER_TOPK = 16
PEER_HALF = 128
PEER_SEL = PEER_HEADS * PEER_TOPK
N_MOD = 6
DEPTH = 1
ALPHA = (2.0 * DEPTH) ** 0.25
EPS = 1e-5

LANES = 128
SUBLANES = 8
VMEM_LIMIT_BYTES = 56 * 1024 * 1024
D_PROJ = 4096
NEG_INF = float("-inf")


def _silu(x):
    return x * jax.nn.sigmoid(x)


def _gelu(x):
    return 0.5 * x * (1.0 + lax.erf(x * (2.0 ** -0.5)))


def _dot(a, b):
    return jnp.dot(a, b, preferred_element_type=f32)


def _dot_nt(a, b):
    return lax.dot_general(a, b, (((1,), (1,)), ((), ())), preferred_element_type=f32)


def _layer_norm(x, g, b):
    mu = jnp.mean(x, axis=-1, keepdims=True)
    xc = x - mu
    var = jnp.mean(xc * xc, axis=-1, keepdims=True)
    return xc * lax.rsqrt(var + EPS) * g + b


def _rms_norm(x, g):
    return x * lax.rsqrt(jnp.mean(x * x, axis=-1, keepdims=True) + EPS) * g


ADA_TN = 1024


def _ada_kernel(c_ref, w_ref, b_ref, o_ref):
    sc = _silu(c_ref[...])
    o_ref[...] = _dot(sc.astype(bf16), w_ref[...].astype(bf16)) + b_ref[...]


def _ada(c_pad, w_ada, b_ada):
    n = w_ada.shape[1]
    return pl.pallas_call(
        _ada_kernel,
        grid=(n // ADA_TN,),
        in_specs=[pl.BlockSpec((SUBLANES, D_MODEL), lambda j: (0, 0)),
                  pl.BlockSpec((D_MODEL, ADA_TN), lambda j: (0, j)),
                  pl.BlockSpec((1, ADA_TN), lambda j: (0, j))],
        out_specs=pl.BlockSpec((SUBLANES, ADA_TN), lambda j: (0, j)),
        out_shape=jax.ShapeDtypeStruct((SUBLANES, n), f32),
        compiler_params=pltpu.CompilerParams(vmem_limit_bytes=VMEM_LIMIT_BYTES),
        name="ada",
    )(c_pad, w_ada, b_ada)


INPROJ_TM = 512
INPROJ_TN = 1024


def _inproj_kernel(x_ref, sh_ref, sc_ref, w_ref, o_ref):
    h = (x_ref[...] * (1.0 + sc_ref[0]) + sh_ref[0]).astype(bf16)
    for n in range(D_PROJ // INPROJ_TN):
        cols = pl.ds(n * INPROJ_TN, INPROJ_TN)
        o_ref[:, cols] = _dot(h, w_ref[:, cols])


def _inproj(x2, mod3, w_cat, seq):
    t = x2.shape[0]
    per_b = seq // INPROJ_TM
    return pl.pallas_call(
        _inproj_kernel,
        grid=(t // INPROJ_TM,),
        in_specs=[pl.BlockSpec((INPROJ_TM, D_MODEL), lambda i: (i, 0)),
                  pl.BlockSpec((1, 1, D_MODEL), lambda i: (i // per_b, 0, 0)),
                  pl.BlockSpec((1, 1, D_MODEL), lambda i: (i // per_b, 0, 1)),
                  pl.BlockSpec((D_MODEL, D_PROJ), lambda i: (0, 0), pipeline_mode=pl.Buffered(1))],
        out_specs=pl.BlockSpec((INPROJ_TM, D_PROJ), lambda i: (i, 0)),
        out_shape=jax.ShapeDtypeStruct((t, D_PROJ), f32),
        compiler_params=pltpu.CompilerParams(vmem_limit_bytes=VMEM_LIMIT_BYTES),
        name="inproj",
    )(x2, mod3, mod3, w_cat)


CONV_TAIL = SUBLANES


def _split3(a):
    hi = a.astype(bf16)
    r1 = a - hi.astype(f32)
    mid = r1.astype(bf16)
    lo = (r1 - mid.astype(f32)).astype(bf16)
    return hi, mid, lo


def _pair_select(lane_lo, even, odd):
    return jnp.where(lane_lo, even, odd)


def _mixer_kernel(sink_ref, z_ref, xs_ref, q_ref, bc_ref, kvdt_ref, cw_ref, cb_ref, dtb_ref, alog_ref,
                  dskip_ref, gssm_ref, gatt_ref, o_ref, ext_scr, st_scr, kp_scr, vp_scr):
    c = pl.program_id(1)
    L = CHUNK

    @pl.when(c == 0)
    def _():
        ext_scr[pl.ds(0, CONV_TAIL), :] = jnp.zeros((CONV_TAIL, D_CONV), f32)
        st_scr[...] = jnp.zeros(st_scr.shape, f32)
        kp_scr[...] = jnp.zeros(kp_scr.shape, f32)
        vp_scr[...] = jnp.zeros(vp_scr.shape, f32)

    ext_scr[pl.ds(CONV_TAIL, L), pl.ds(0, D_SSM)] = xs_ref[...]
    ext_scr[pl.ds(CONV_TAIL, L), pl.ds(D_SSM, D_BC)] = bc_ref[...]
    conv = jnp.zeros((L, D_CONV), f32) + cb_ref[...]
    for w in range(CONV_WIDTH):
        off = CONV_TAIL - (CONV_WIDTH - 1) + w
        conv = conv + ext_scr[pl.ds(off, L), :] * cw_ref[pl.ds(w, 1), :]
    tail = ext_scr[pl.ds(L, CONV_TAIL), :]
    ext_scr[pl.ds(0, CONV_TAIL), :] = tail
    xbc = _silu(conv)
    xs = xbc[:, :D_SSM]

    kvdt = kvdt_ref[...]
    k_cur = kvdt[:, 0:D_KV]
    v_cur = kvdt[:, D_KV:2 * D_KV]
    dt_raw = kvdt[:, 2 * D_KV:3 * D_KV]

    xdt = dt_raw + dtb_ref[...]
    dt = jnp.maximum(xdt, 0.0) + jnp.log1p(jnp.exp(-jnp.abs(xdt)))
    a = dt * (-jnp.exp(alog_ref[...]))
    row_i = lax.broadcasted_iota(i32, (L, L), 0)
    col_i = lax.broadcasted_iota(i32, (L, L), 1)
    causal = row_i >= col_i
    tri = jnp.where(causal, 1.0, 0.0).astype(bf16)
    a_hi, a_mid, a_lo = _split3(a)
    a_cum = _dot(tri, a_hi) + _dot(tri, a_mid) + _dot(tri, a_lo)
    a_cum_t = a_cum.T
    dt_t = dt.T
    a_last = a_cum[L - 1:L, :]
    e_cum = jnp.exp(a_cum)
    to_end = jnp.exp(a_last - a_cum) * dt
    cdec = jnp.exp(a_last)
    lane_lo = lax.broadcasted_iota(i32, (L, LANES), 1) < HEADDIM
    lane_lo_row = lax.broadcasted_iota(i32, (1, LANES), 1) < HEADDIM

    y_parts = []
    for g in range(SSM_GROUPS):
        bm = xbc[:, D_SSM + g * SSM_STATE:D_SSM + (g + 1) * SSM_STATE]
        cm = xbc[:, D_SSM + (SSM_GROUPS + g) * SSM_STATE:D_SSM + (SSM_GROUPS + g + 1) * SSM_STATE]
        bm16 = bm.astype(bf16)
        cm16 = cm.astype(bf16)
        cb = _dot_nt(cm16, bm16)
        s_prev = st_scr[g]
        y_off = _dot(cm16, s_prev.astype(bf16))
        xs_scaled = []
        cd_rows = []
        for p in range(4):
            pair = g * 4 + p
            x_pair = xs[:, pair * LANES:(pair + 1) * LANES]
            x16 = x_pair.astype(bf16)
            yy = []
            for half in range(2):
                hd = 2 * pair + half
                seg = a_cum[:, hd:hd + 1] - a_cum_t[hd:hd + 1, :]
                decay = jnp.exp(jnp.where(causal, seg, NEG_INF))
                wmat = cb * decay * dt_t[hd:hd + 1, :]
                yy.append(_dot(wmat.astype(bf16), x16))
            hd0, hd1 = 2 * pair, 2 * pair + 1
            y_diag = _pair_select(lane_lo, yy[0], yy[1])
            e_pair = _pair_select(lane_lo, e_cum[:, hd0:hd0 + 1], e_cum[:, hd1:hd1 + 1])
            y_parts.append(y_diag + y_off[:, p * LANES:(p + 1) * LANES] * e_pair)
            te_pair = _pair_select(lane_lo, to_end[:, hd0:hd0 + 1], to_end[:, hd1:hd1 + 1])
            xs_scaled.append((x_pair * te_pair).astype(bf16))
            cd_rows.append(_pair_select(lane_lo_row, cdec[:, hd0:hd0 + 1], cdec[:, hd1:hd1 + 1]))
        xs_sc = jnp.concatenate(xs_scaled, axis=1)
        s_new = _dot(bm.T.astype(bf16), xs_sc)
        st_scr[g] = jnp.concatenate(cd_rows, axis=1) * s_prev + s_new
    y = jnp.concatenate(y_parts, axis=1) + dskip_ref[...] * xs
    y_ssd = _rms_norm(y * _silu(z_ref[...]), gssm_ref[...])
    o_ref[:, pl.ds(0, D_SSM)] = y_ssd.astype(o_ref.dtype)

    kcat = jnp.concatenate([kp_scr[...], k_cur], axis=0)
    vcat = jnp.concatenate([vp_scr[...], v_cur], axis=0)
    kp_scr[...] = k_cur
    vp_scr[...] = v_cur
    kroll = pltpu.roll(kcat, HEADDIM, axis=1)
    vroll = pltpu.roll(vcat, HEADDIM, axis=1)
    lane_lo2 = lax.broadcasted_iota(i32, (2 * L, LANES), 1) < HEADDIM
    qi = lax.broadcasted_iota(i32, (L, 2 * L), 0)
    kj = lax.broadcasted_iota(i32, (L, 2 * L), 1)
    dist = qi + L - kj
    valid = (dist >= 0) & (dist < L) & ((c > 0) | (kj >= L))
    att_parts = []
    for kvh in range(ATT_KV_HEADS):
        if kvh == 0:
            kd = jnp.where(lane_lo2, kcat, kroll).astype(bf16)
            vd = jnp.where(lane_lo2, vcat, vroll).astype(bf16)
        else:
            kd = jnp.where(lane_lo2, kroll, kcat).astype(bf16)
            vd = jnp.where(lane_lo2, vroll, vcat).astype(bf16)
        for p in range(4):
            pair = kvh * 4 + p
            q_pair = q_ref[:, pl.ds(pair * LANES, LANES)] * (HEADDIM ** -0.5)
            oo = []
            for half in range(2):
                hd = 2 * pair + half
                keep = lane_lo if half == 0 else jnp.logical_not(lane_lo)
                qm = jnp.where(keep, q_pair, 0.0).astype(bf16)
                s = _dot_nt(qm, kd)
                s = jnp.where(valid, s, NEG_INF)
                sink = sink_ref[hd]
                m = jnp.maximum(jnp.max(s, axis=1, keepdims=True), sink)
                pr = jnp.exp(s - m)
                denom = jnp.sum(pr, axis=1, keepdims=True) + jnp.exp(sink - m)
                oo.append(_dot((pr / denom).astype(bf16), vd))
            att_parts.append(_pair_select(lane_lo, oo[0], oo[1]))
    att = jnp.concatenate(att_parts, axis=1)
    o_ref[:, pl.ds(D_SSM, D_ATT)] = _rms_norm(att, gatt_ref[...]).astype(o_ref.dtype)


def _mixer(proj, sinks, conv_w8, conv_b, dtb, alog, dskip, gssm, gatt, batch, seq):
    t = proj.shape[0]
    nc = seq // CHUNK

    def tok(col):
        return lambda b, c: (b * nc + c, col)

    const = lambda b, c: (0, 0)
    return pl.pallas_call(
        _mixer_kernel,
        grid=(batch, nc),
        in_specs=[pl.BlockSpec(memory_space=pltpu.SMEM),
                  pl.BlockSpec((CHUNK, D_SSM), tok(0)),
                  pl.BlockSpec((CHUNK, D_SSM), tok(1)),
                  pl.BlockSpec((CHUNK, D_ATT), tok(2)),
                  pl.BlockSpec((CHUNK, D_BC), tok(6)),
                  pl.BlockSpec((CHUNK, D_BC), tok(7)),
                  pl.BlockSpec((SUBLANES, D_CONV), const),
                  pl.BlockSpec((1, D_CONV), const),
                  pl.BlockSpec((1, LANES), const),
                  pl.BlockSpec((1, LANES), const),
                  pl.BlockSpec((1, D_SSM), const),
                  pl.BlockSpec((1, D_SSM), const),
                  pl.BlockSpec((1, D_ATT), const)],
        out_specs=pl.BlockSpec((CHUNK, D_MODEL), lambda b, c: (b * nc + c, 0)),
        out_shape=jax.ShapeDtypeStruct((t, D_MODEL), bf16),
        scratch_shapes=[pltpu.VMEM((CONV_TAIL + CHUNK, D_CONV), f32),
                        pltpu.VMEM((SSM_GROUPS, SSM_STATE, D_SSM // SSM_GROUPS), f32),
                        pltpu.VMEM((CHUNK, D_KV), f32),
                        pltpu.VMEM((CHUNK, D_KV), f32)],
        compiler_params=pltpu.CompilerParams(
            dimension_semantics=("arbitrary", "arbitrary"),
            vmem_limit_bytes=VMEM_LIMIT_BYTES),
        name="mixer",
    )(sinks, proj, proj, proj, proj, proj, conv_w8, conv_b, dtb, alog, dskip, gssm, gatt)


OUT_TM = 256


PEER_UNITS = 2 * PEER_HEADS


def _outproj_kernel(y_ref, w_ref, x_ref, gate_ref, g_ref, b_ref, sh_ref, sc_ref, wq_ref,
                    x1_ref, h2_ref, q_ref):
    y = _dot(y_ref[...], w_ref[...])
    x1 = _layer_norm(ALPHA * x_ref[...] + gate_ref[0] * y, g_ref[...], b_ref[...])
    x1_ref[...] = x1
    h2 = x1 * (1.0 + sc_ref[0]) + sh_ref[0]
    h2_ref[...] = h2
    q = _dot(h2.astype(bf16), wq_ref[...])
    for u in range(PEER_UNITS):
        q_ref[u] = q[:, u * PEER_HALF:(u + 1) * PEER_HALF].astype(bf16)


def _outproj(ycat, w_out, x2, mod3, ln_g, ln_b, w_q, seq):
    t = x2.shape[0]
    per_b = seq // OUT_TM
    tok = lambda i: (i, 0)
    const = lambda i: (0, 0)

    def modspec(k):
        return pl.BlockSpec((1, 1, D_MODEL), lambda i: (i // per_b, 0, k))

    def weight():
        return pl.BlockSpec((D_MODEL, D_MODEL), const, pipeline_mode=pl.Buffered(1))

    return pl.pallas_call(
        _outproj_kernel,
        grid=(t // OUT_TM,),
        in_specs=[pl.BlockSpec((OUT_TM, D_MODEL), tok),
                  weight(),
                  pl.BlockSpec((OUT_TM, D_MODEL), tok),
                  modspec(2), pl.BlockSpec((1, D_MODEL), const), pl.BlockSpec((1, D_MODEL), const),
                  modspec(3), modspec(4),
                  weight()],
        out_specs=[pl.BlockSpec((OUT_TM, D_MODEL), tok), pl.BlockSpec((OUT_TM, D_MODEL), tok),
                   pl.BlockSpec((PEER_UNITS, OUT_TM, PEER_HALF), lambda i: (0, i, 0))],
        out_shape=[jax.ShapeDtypeStruct((t, D_MODEL), f32), jax.ShapeDtypeStruct((t, D_MODEL), f32),
                   jax.ShapeDtypeStruct((PEER_UNITS, t, PEER_HALF), bf16)],
        compiler_params=pltpu.CompilerParams(vmem_limit_bytes=VMEM_LIMIT_BYTES),
        name="outproj",
    )(ycat, w_out, x2, mod3, ln_g, ln_b, mod3, mod3, w_q)


ORDER_LAST = 2 ** 30


def _topk_rows(s, k, order, payload=None, every=None, between=None):
    vals, outs = [], []
    for it in range(k):
        if between is not None and it % every == 0:
            between()
        m = jnp.max(s, axis=0, keepdims=True)
        am = jnp.min(jnp.where(s == m, order, ORDER_LAST), axis=0, keepdims=True)
        hit = order == am
        vals.append(m)
        if payload is None:
            outs.append(am)
        else:
            outs.append(jnp.max(jnp.where(hit, payload, -1), axis=0, keepdims=True))
        s = jnp.where(hit, NEG_INF, s)
    return jnp.concatenate(vals, axis=0), jnp.concatenate(outs, axis=0)


def _candidate_slabs(v1, i1, v2, i2):
    tm = v1.shape[1]
    cands, orders, experts = [], [], []

    def slab(vals, flat, eid, live):
        cands.append(jnp.where(live, vals, NEG_INF))
        orders.append(jnp.where(live, flat, ORDER_LAST))
        experts.append(eid)

    for k1, rows, n_live in ((0, 16, 16), (1, 8, 8), (2, 8, 5), (3, 8, 4)):
        k2 = lax.broadcasted_iota(i32, (rows, tm), 0)
        slab(v1[k1:k1 + 1] + v2[0:rows], k1 * PEER_TOPK + k2,
             i1[k1:k1 + 1] * PEER_KEYS + i2[0:rows], k2 < n_live)
    for k2, rows, hi in ((0, 16, 16), (1, 8, 8), (2, 8, 5)):
        k1 = lax.broadcasted_iota(i32, (rows, tm), 0)
        slab(v1[0:rows] + v2[k2:k2 + 1], k1 * PEER_TOPK + k2,
             i1[0:rows] * PEER_KEYS + i2[k2:k2 + 1], (k1 >= 4) & (k1 < hi))
    return jnp.concatenate(cands, axis=0), jnp.concatenate(orders, axis=0), jnp.concatenate(experts, axis=0)


ROUTE_HOOK_EVERY = 4
ROUTE_HOOK_CALLS = 2 * PEER_TOPK // ROUTE_HOOK_EVERY


def _route_head(q_ref, keys_ref, h, between=None):
    tm = q_ref.shape[1]
    key_id = lax.broadcasted_iota(i32, (PEER_KEYS, tm), 0)
    vals, idxs = [], []
    for cpart in range(2):
        unit = h * 2 + cpart
        sc = _dot_nt(keys_ref[unit], q_ref[unit])
        v, ix = _topk_rows(sc, PEER_TOPK, key_id, every=ROUTE_HOOK_EVERY, between=between)
        vals.append(v)
        idxs.append(ix)
    cand, order, cidx = _candidate_slabs(vals[0], idxs[0], vals[1], idxs[1])
    best, expert = _topk_rows(cand, PEER_TOPK, order, payload=cidx)
    ex = jnp.exp(best - jnp.max(best, axis=0, keepdims=True))
    return expert, ex / jnp.sum(ex, axis=0, keepdims=True)


EXP_TB = 128
EXP_SUB = 8
EXP_NSUB = EXP_TB // EXP_SUB
EXP_ROWS = EXP_SUB * PEER_SEL
D_UV = 2 * D_MODEL


EXP_PART = 32
EXP_PARTS = PEER_SEL // EXP_PART
N_BURSTS = EXP_SUB * EXP_PARTS
ROUTE_SUBS = PEER_HEADS
IDX_PUBLISH = ROUTE_SUBS
IDX_READY = EXP_NSUB - 2


def _peer_kernel(q_ref, keys_ref, h_ref, x1_ref, gate_ref, g_ref, b_ref, uv_hbm, o_ref,
                 buf, y_scr, e_scr, et_scr, gate_scr, idx_smem, sem, idx_sem):
    step = pl.program_id(0)
    routes = step < pl.num_programs(0) - 1
    route_buf = step % 2
    gather_buf = 1 - route_buf

    def burst(b, ids_buf, sub, slot):
        tk, part = divmod(b, EXP_PARTS)
        for k in range(part * EXP_PART, (part + 1) * EXP_PART):
            e = idx_smem[ids_buf, sub * EXP_SUB + tk, k]
            pltpu.make_async_copy(uv_hbm.at[e], buf.at[slot, pl.ds(tk * PEER_SEL + k, 1)],
                                  sem.at[slot, tk]).start()

    def wait_token(slot, tk):
        rows = pl.ds(tk * PEER_SEL, PEER_SEL)
        pltpu.make_async_copy(buf.at[slot, rows], buf.at[slot, rows], sem.at[slot, tk]).wait()

    def idx_copy():
        return pltpu.make_async_copy(et_scr, idx_smem.at[route_buf], idx_sem)

    def route_head(h, between=None):
        expert, gate = _route_head(q_ref, keys_ref, h, between=between)
        rows = pl.ds(pl.multiple_of(h * PEER_TOPK, PEER_TOPK), PEER_TOPK)
        e_scr[rows, :] = expert
        gate_scr[route_buf, rows, :] = gate

    def publish_ids():
        et_scr[...] = e_scr[...].T
        idx_copy().start()

    @pl.when(step == 0)
    def _():
        def head(h, carry):
            route_head(h)
            return carry
        lax.fori_loop(0, PEER_HEADS, head, 0)
        publish_ids()
        idx_copy().wait()
        for b in range(N_BURSTS):
            burst(b, route_buf, 0, 0)

    @pl.when(step >= 1)
    def _():
        def sub_block(j, slot):
            in_block = j + 1 < EXP_NSUB
            ids_buf = jnp.where(in_block | jnp.logical_not(routes), gather_buf, route_buf)
            nxt = jnp.where(in_block, j + 1, 0)
            routing_sub = routes & (j < ROUTE_SUBS)

            @pl.when(routing_sub)
            def _():
                pending = iter(range(ROUTE_HOOK_CALLS))
                route_head(j, between=lambda: burst(next(pending), gather_buf, j + 1, 1 - slot))

            @pl.when(jnp.logical_not(routing_sub))
            def _():
                for b in range(ROUTE_HOOK_CALLS):
                    burst(b, ids_buf, nxt, 1 - slot)

            @pl.when(routes & (j == IDX_PUBLISH))
            def _():
                publish_ids()

            @pl.when(routes & (j == IDX_READY))
            def _():
                idx_copy().wait()

            gs = pltpu.roll(gate_scr[gather_buf], (EXP_TB - j * EXP_SUB) & (EXP_TB - 1), axis=1)
            per_token = (N_BURSTS - ROUTE_HOOK_CALLS) // EXP_SUB
            for tk in range(EXP_SUB):
                for b in range(ROUTE_HOOK_CALLS + tk * per_token, ROUTE_HOOK_CALLS + (tk + 1) * per_token):
                    burst(b, ids_buf, nxt, 1 - slot)
                wait_token(slot, tk)
                rows = pl.ds(tk * PEER_SEL, PEER_SEL)
                tok_row = pl.ds(j * EXP_SUB + tk, 1)
                a = jnp.sum(buf[slot, rows, pl.ds(0, D_MODEL)] * h_ref[tok_row, :], axis=1, keepdims=True)
                wgt = gs[:, tk:tk + 1] * _gelu(a)
                y_scr[tok_row, :] = jnp.sum(buf[slot, rows, pl.ds(D_MODEL, D_MODEL)] * wgt, axis=0,
                                            keepdims=True)

        def sub_block_pair(jj, carry):
            sub_block(2 * jj, 0)
            sub_block(2 * jj + 1, 1)
            return carry

        lax.fori_loop(0, EXP_NSUB // 2, sub_block_pair, 0)

        @pl.when(jnp.logical_not(routes))
        def _():
            for tk in range(EXP_SUB):
                wait_token(0, tk)

        o_ref[...] = _layer_norm(ALPHA * x1_ref[...] + gate_ref[0] * y_scr[...], g_ref[...], b_ref[...])


def _peer(q_units, keys, h2, x1, mod3, ln_g, ln_b, uv_tab, seq):
    t = h2.shape[0]
    n_blocks = t // EXP_TB
    per_b = seq // EXP_TB
    assert EXP_NSUB % 2 == 0, "the cross-step prefetch assumes every grid step starts on slot 0"
    assert ROUTE_SUBS <= IDX_PUBLISH < IDX_READY < EXP_NSUB - 1
    gathered = lambda s: jnp.maximum(s - 1, 0)
    tok = lambda s: (gathered(s), 0)
    const = lambda s: (0, 0)
    return pl.pallas_call(
        _peer_kernel,
        grid=(n_blocks + 1,),
        in_specs=[pl.BlockSpec((PEER_UNITS, EXP_TB, PEER_HALF), lambda s: (0, jnp.minimum(s, n_blocks - 1), 0)),
                  pl.BlockSpec((PEER_UNITS, PEER_KEYS, PEER_HALF), lambda s: (0, 0, 0)),
                  pl.BlockSpec((EXP_TB, D_MODEL), tok),
                  pl.BlockSpec((EXP_TB, D_MODEL), tok),
                  pl.BlockSpec((1, 1, D_MODEL), lambda s: (gathered(s) // per_b, 0, 5)),
                  pl.BlockSpec((1, D_MODEL), const),
                  pl.BlockSpec((1, D_MODEL), const),
                  pl.BlockSpec(memory_space=pl.ANY)],
        out_specs=pl.BlockSpec((EXP_TB, D_MODEL), tok),
        out_shape=jax.ShapeDtypeStruct((t, D_MODEL), f32),
        scratch_shapes=[pltpu.VMEM((2, EXP_ROWS, D_UV), f32),
                        pltpu.VMEM((EXP_TB, D_MODEL), f32),
                        pltpu.VMEM((PEER_SEL, EXP_TB), i32),
                        pltpu.VMEM((EXP_TB, PEER_SEL), i32),
                        pltpu.VMEM((2, PEER_SEL, EXP_TB), f32),
                        pltpu.SMEM((2, EXP_TB, PEER_SEL), i32),
                        pltpu.SemaphoreType.DMA((2, EXP_SUB)),
                        pltpu.SemaphoreType.DMA(())],
        compiler_params=pltpu.CompilerParams(
            dimension_semantics=("arbitrary",),
            vmem_limit_bytes=VMEM_LIMIT_BYTES),
        name="peer",
    )(q_units, keys, h2, x1, mod3, ln_g, ln_b, uv_tab)


def _regroup_w_in(w):
    o_xbc = D_SSM
    o_dt = o_xbc + D_CONV
    o_q = o_dt + SSM_HEADS
    o_k = o_q + D_ATT
    o_v = o_k + D_KV
    pad = jnp.zeros((w.shape[0], D_PROJ - (2 * D_SSM + D_ATT + D_BC + 2 * D_KV + SSM_HEADS)), w.dtype)
    return jnp.concatenate([w[:, :D_SSM], w[:, o_xbc:o_xbc + D_SSM], w[:, o_q:o_k],
                            w[:, o_xbc + D_SSM:o_dt], w[:, o_k:o_v], w[:, o_v:o_v + D_KV],
                            w[:, o_dt:o_q], pad], axis=1)


def _pad_lanes(v):
    return jnp.pad(v, (0, LANES - v.shape[0]))[None, :]


def kernel(x, c, w_ada, b_ada, w_in, conv_w, conv_b, dt_bias, a_log, d_skip, ssm_norm_g, attn_sinks, attn_norm_g, w_out, ln1_g, ln1_b, peer_w_q, peer_sub_keys, peer_u, peer_v, ln2_g, ln2_b):
    batch, seq, d = x.shape
    t = batch * seq
    x2 = x.reshape(t, d)
    for l in range(DEPTH):
        c_pad = jnp.pad(c, ((0, SUBLANES - batch), (0, 0)))
        mod = _ada(c_pad, w_ada[l], b_ada[l][None, :])
        mod3 = mod[:batch].reshape(batch, 1, N_MOD * d)
        proj = _inproj(x2, mod3, _regroup_w_in(w_in[l].astype(bf16)), seq)
        conv_w8 = jnp.pad(conv_w[l], ((0, SUBLANES - CONV_WIDTH), (0, 0)))
        ycat = _mixer(proj, attn_sinks[l], conv_w8, conv_b[l][None, :], _pad_lanes(dt_bias[l]),
                      _pad_lanes(a_log[l]), jnp.repeat(d_skip[l], HEADDIM)[None, :],
                      ssm_norm_g[l][None, :], attn_norm_g[l][None, :], batch, seq)
        x1, h2, q_units = _outproj(ycat, w_out[l].astype(bf16), x2, mod3, ln1_g[l][None, :], ln1_b[l][None, :],
                                   peer_w_q[l].astype(bf16), seq)
        keys = peer_sub_keys[l].astype(bf16).reshape(PEER_UNITS, PEER_KEYS, PEER_HALF)
        uv_tab = jnp.concatenate([peer_u[l], peer_v[l]], axis=1)[:, None, :]
        x2 = _peer(q_units, keys, h2, x1, mod3, ln2_g[l][None, :], ln2_b[l][None, :], uv_tab, seq)
    return x2.reshape(batch, seq, d)
```
